```python
import jax, jax.numpy as jnp
from jax import lax
import numpy as np


D_MODEL = 1024
BATCH = 8
SEQ = 2048
DEPTH = 4
DEC_BATCH = 8
DEC_SEQ = 4096
PAST_LEN = 128

MIX_WIDTH = D_MODEL
FNET_WIDTH = MIX_WIDTH // 2
FNET_GROUPS = 4
FNET_GROUP_DIM = FNET_WIDTH // FNET_GROUPS
HGRN_WIDTH = MIX_WIDTH - FNET_WIDTH
HGRN_HEADS = 4
HGRN_DK = 128
HGRN_DV = HGRN_WIDTH // HGRN_HEADS
QK_WIDTH = HGRN_HEADS * HGRN_DK
V_WIDTH = HGRN_HEADS * HGRN_DV
IN_WIDTH = FNET_WIDTH + 3 * QK_WIDTH + 2 * V_WIDTH
CHUNK = 64
N_EXPERTS = 16
EC_CAPACITY_FACTOR = 2
D_FF_EXPERT = 2 * D_MODEL
RMS_EPS = 1e-6

kernel_name = "fnet_hgrn2_expert_choice_hybrid_encoder"


def rms_norm(x, w):
    xf = x.astype(jnp.float32)
    y = xf * lax.rsqrt(jnp.mean(xf * xf, axis=-1, keepdims=True) + RMS_EPS)
    return (y * w.astype(jnp.float32)).astype(x.dtype)


def lower_bounds(lb_param):
    p = jax.nn.softmax(lb_param.astype(jnp.float32), axis=0)
    return jnp.maximum(jnp.cumsum(p, axis=0) - p[0:1], 0.0)


def hgrn2_gates(z, lb):
    zf = z.astype(jnp.float32)
    k = (1.0 - lb) * jax.nn.sigmoid(-zf)
    logf = jnp.logaddexp(jnp.log(lb), jnp.log1p(-lb) + jax.nn.log_sigmoid(zf))
    shape = z.shape[:2] + (HGRN_HEADS, HGRN_DK)
    return k.reshape(shape), logf.reshape(shape)


def chunked_gated_recurrence(q, k, v, logf):
    bsz, seq, nh, dk = q.shape
    dv = v.shape[-1]
    n_chunks = seq // CHUNK

    def to_chunks(a):
        return a.reshape(bsz, n_chunks, CHUNK, nh, a.shape[-1]).transpose(1, 0, 3, 2, 4)

    qc, kc, vc, gc = to_chunks(q), to_chunks(k), to_chunks(v), to_chunks(logf)
    causal_in_chunk = jnp.tril(jnp.ones((CHUNK, CHUNK), dtype=bool))[None, None, :, :, None]

    def step(state, inp):
        qb, kb, vb, gb = inp
        b = jnp.cumsum(gb, axis=2)
        diff = b[:, :, :, None, :] - b[:, :, None, :, :]
        decay = jnp.exp(jnp.where(causal_in_chunk, diff, -jnp.inf))
        scores = jnp.einsum("bhtd,bhsd,bhtsd->bhts", qb, kb, decay)
        o = (jnp.einsum("bhts,bhsv->bhtv", scores, vb)
             + jnp.einsum("bhtd,bhdv->bhtv", qb * jnp.exp(b), state))
        b_last = b[:, :, -1]
        state = (jnp.exp(b_last)[..., None] * state
                 + jnp.einsum("bhsd,bhsv->bhdv", kb * jnp.exp(b_last[:, :, None] - b), vb))
        return state, o

    state0 = jnp.zeros((bsz, nh, dk, dv), jnp.float32)
    _, oc = lax.scan(step, state0, (qc, kc, vc, gc))
    return oc.transpose(1, 0, 3, 2, 4).reshape(bsz, seq, nh, dv)


def hybrid_mixer(h, w_in, w_fnet, lb_f, lb_b, out_norm, w_out):
    bsz, seq, _ = h.shape
    proj = h @ w_in
    cuts = np.cumsum([FNET_WIDTH, QK_WIDTH, QK_WIDTH, QK_WIDTH, V_WIDTH]).tolist()
    u, q, zf, zb, iv, g = jnp.split(proj, cuts, axis=-1)

    ug = u.reshape(bsz, seq, FNET_GROUPS, FNET_GROUP_DIM).astype(jnp.float32)
    mixed = jnp.fft.fft2(ug, axes=(1, 3), norm="ortho").real.astype(h.dtype)
    y_fourier = jnp.einsum("blgc,gcd->blgd", mixed, w_fnet).reshape(bsz, seq, FNET_WIDTH)

    qh = jax.nn.silu(q).reshape(bsz, seq, HGRN_HEADS, HGRN_DK)
    vh = iv.reshape(bsz, seq, HGRN_HEADS, HGRN_DV)
    k_f, logf_f = hgrn2_gates(zf, lb_f)
    k_b, logf_b = hgrn2_gates(zb, lb_b)
    o_fwd = chunked_gated_recurrence(qh, k_f, vh, logf_f)
    o_bwd = jnp.flip(chunked_gated_recurrence(jnp.flip(qh, 1), jnp.flip(k_b, 1),
                                              jnp.flip(vh, 1), jnp.flip(logf_b, 1)), 1)
    gate = jax.nn.silu(g.reshape(bsz, seq, HGRN_HEADS, HGRN_DV).astype(jnp.float32))
    y_rec = (rms_norm(o_fwd + o_bwd, out_norm) * gate).astype(h.dtype).reshape(bsz, seq, V_WIDTH)

    return jnp.concatenate([y_fourier, y_rec], axis=-1) @ w_out


def expert_choice_ffn(h, w_router, w_gate, w_up, w_down):
    bsz, seq, d = h.shape
    n_tok = bsz * seq
    cap = EC_CAPACITY_FACTOR * n_tok // N_EXPERTS
    xt = h.reshape(n_tok, d)
    affinity = jax.nn.softmax((xt @ w_router).astype(jnp.float32), axis=-1)
    g, idx = lax.top_k(affinity.T, cap)
    xe = xt[idx]
    he = jax.nn.silu(jnp.einsum("ecd,edf->ecf", xe, w_gate)) * jnp.einsum("ecd,edf->ecf", xe, w_up)
    ye = (jnp.einsum("ecf,efd->ecd", he, w_down) * g[..., None]).astype(xt.dtype)
    out = jnp.zeros_like(xt).at[idx.reshape(-1)].add(ye.reshape(-1, d))
    return out.reshape(bsz, seq, d)


def encoder_trunk(x, lb_f_all, lb_b_all, norm_mix, w_in, w_fnet, hgrn_out_norm, w_out,
                  norm_ffn, w_router, w_gate, w_up, w_down, norm_final):
    for layer in range(DEPTH):
        h = rms_norm(x, norm_mix[layer])
        x = x + hybrid_mixer(h, w_in[layer], w_fnet[layer], lb_f_all[layer], lb_b_all[layer],
                             hgrn_out_norm[layer], w_out[layer])
        h = rms_norm(x, norm_ffn[layer])
        x = x + expert_choice_ffn(h, w_router[layer], w_gate[layer], w_up[layer], w_down[layer])
    return rms_norm(x, norm_final)


def setup_inputs(seed: int = 0) -> dict:
    key = jax.random.key(seed)
    ks = jax.random.split(key, 15)
    f32 = jnp.float32

    def dense(k, shape, fan_in):
        return jax.random.normal(k, shape, f32) * fan_in ** -0.5

    def gain(k, shape):
        return 1.0 + 0.02 * jax.random.normal(k, shape, f32)

    return {
        "x_prompt": jax.random.normal(ks[0], (BATCH, SEQ, D_MODEL), f32),
        "x_sample": jax.random.normal(ks[1], (DEC_BATCH, DEC_SEQ, D_MODEL), f32),
        "norm_mix": gain(ks[2], (DEPTH, D_MODEL)),
        "w_in": dense(ks[3], (DEPTH, D_MODEL, IN_WIDTH), D_MODEL),
        "w_fnet": dense(ks[4], (DEPTH, FNET_GROUPS, FNET_GROUP_DIM, FNET_GROUP_DIM), FNET_GROUP_DIM),
        "lb_fwd": 1.0 + 0.1 * jax.random.normal(ks[5], (DEPTH, QK_WIDTH), f32),
        "lb_bwd": 1.0 + 0.1 * jax.random.normal(ks[6], (DEPTH, QK_WIDTH), f32),
        "hgrn_out_norm": gain(ks[7], (DEPTH, HGRN_DV)),
        "w_out": dense(ks[8], (DEPTH, MIX_WIDTH, D_MODEL), MIX_WIDTH),
        "norm_ffn": gain(ks[9], (DEPTH, D_MODEL)),
        "w_router": dense(ks[10], (DEPTH, D_MODEL, N_EXPERTS), D_MODEL),
        "w_gate": dense(ks[11], (DEPTH, N_EXPERTS, D_MODEL, D_FF_EXPERT), D_MODEL),
        "w_up": dense(ks[12], (DEPTH, N_EXPERTS, D_MODEL, D_FF_EXPERT), D_MODEL),
        "w_down": dense(ks[13], (DEPTH, N_EXPERTS, D_FF_EXPERT, D_MODEL), D_FF_EXPERT),
        "norm_final": gain(ks[14], (D_MODEL,)),
    }


def reference(x_prompt, x_sample, norm_mix, w_in, w_fnet, lb_fwd, lb_bwd, hgrn_out_norm, w_out,
              norm_ffn, w_router, w_gate, w_up, w_down, norm_final):
    lb_f_all = lower_bounds(lb_fwd)
    lb_b_all = lower_bounds(lb_bwd)
    y_prompt = encoder_trunk(x_prompt, lb_f_all, lb_b_all, norm_mix, w_in, w_fnet, hgrn_out_norm, w_out,
                             norm_ffn, w_router, w_gate, w_up, w_down, norm_final)
    y_sample = encoder_trunk(x_sample, lb_f_all, lb_b_all, norm_mix, w_in, w_fnet, hgrn_out_norm, w_out,
                             norm_ffn, w_router, w_gate, w_up, w_down, norm_final)
    return (y_prompt, y_sample)
```

```python
import functools

import numpy as np
import jax
import jax.numpy as jnp
from jax import lax
from jax.experimental import pallas as pl
from jax.experimental.pallas import tpu as pltpu

D_MODEL = 1024
DEPTH = 4
FNET_WIDTH = 512
FNET_GROUPS = 4
FNET_GROUP_DIM = 128
HGRN_HEADS = 4
HGRN_DK = 128
HGRN_DV = 128
QK_WIDTH = 512
V_WIDTH = 512
IN_WIDTH = FNET_WIDTH + 3 * QK_WIDTH + 2 * V_WIDTH
N_EXPERTS = 16
EC_CAPACITY_FACTOR = 2
D_FF_EXPERT = 2048
RMS_EPS = 1e-6

V7X_LANES = 128
V7X_SUBLANES = 8
V7X_VMEM_LIMIT_BYTES = 56 * 1024 * 1024

BF16 = jnp.bfloat16
F32 = jnp.float32

CHUNK = 64
SUB = 16
N_SUB = CHUNK // SUB

ROW_TILE = 512
SLOT_TILE = 256
TOKEN_TILE = 256
ROUTE_CHUNK = 128


def _cparams(sem, vmem=V7X_VMEM_LIMIT_BYTES):
    return pltpu.CompilerParams(dimension_semantics=sem, vmem_limit_bytes=vmem)


def _sigmoid(x):
    return 1.0 / (1.0 + jnp.exp(-x))


def _lower_bounds_kernel(p_ref, o_ref):
    p = p_ref[...]
    m = jnp.max(p, axis=0, keepdims=True)
    e = jnp.exp(p - m)
    sm = e / jnp.sum(e, axis=0, keepdims=True)
    acc = None
    rows = []
    for layer in range(DEPTH):
        acc = sm[layer:layer + 1] if acc is None else acc + sm[layer:layer + 1]
        rows.append(jnp.maximum(acc - sm[0:1], 0.0))
    o_ref[...] = jnp.concatenate(rows, axis=0)


def _lower_bounds(lb_param):
    return pl.pallas_call(
        _lower_bounds_kernel,
        out_shape=jax.ShapeDtypeStruct((DEPTH, QK_WIDTH), F32),
        name="lower_bounds",
    )(lb_param)


def _hgrn_gates(z, lb):
    e = jnp.exp(-jnp.abs(z))
    r = 1.0 / (1.0 + e)
    sig_neg = jnp.where(z >= 0, e * r, r)
    k = (1.0 - lb) * sig_neg
    log_sig = jnp.minimum(z, 0.0) - jnp.log1p(e)
    c = jnp.log1p(-lb) + log_sig
    has_lb = lb > 0.0
    a = jnp.log(jnp.where(has_lb, lb, 1.0))
    logf = jnp.where(has_lb, jnp.maximum(a, c) + jnp.log1p(jnp.exp(-jnp.abs(a - c))), c)
    return k, logf


def _inproj_kernel(x_ref, nw_ref, w_ref, lbf_ref, lbb_ref,
                   u_ref, q_ref, kf_ref, kb_ref, lf_ref, lbo_ref, v_ref, g_ref):
    x = x_ref[...]
    ms = jnp.mean(x * x, axis=-1, keepdims=True)
    h = (x * lax.rsqrt(ms + RMS_EPS) * nw_ref[...]).astype(BF16)

    def proj(col0, width):
        return jnp.dot(h, w_ref[:, col0:col0 + width], preferred_element_type=F32)

    u_ref[...] = proj(0, FNET_WIDTH)
    q = proj(FNET_WIDTH, QK_WIDTH)
    q_ref[...] = q * _sigmoid(q)
    kf, lf = _hgrn_gates(proj(FNET_WIDTH + QK_WIDTH, QK_WIDTH), lbf_ref[...])
    kf_ref[...] = kf
    lf_ref[...] = lf
    kb, lb = _hgrn_gates(proj(FNET_WIDTH + 2 * QK_WIDTH, QK_WIDTH), lbb_ref[...])
    kb_ref[...] = kb
    lbo_ref[...] = lb
    v_ref[...] = proj(FNET_WIDTH + 3 * QK_WIDTH, V_WIDTH)
    g = proj(FNET_WIDTH + 3 * QK_WIDTH + V_WIDTH, V_WIDTH)
    g_ref[...] = g * _sigmoid(g)


def _inproj(x, norm_w, w_in_bf16, lbf, lbb, layer):
    n_tok = x.shape[0]
    tm = min(ROW_TILE, n_tok)
    row = lambda i: (i, 0)
    lay3 = lambda i: (layer, 0, 0)
    out_sds = jax.ShapeDtypeStruct((n_tok, QK_WIDTH), F32)
    return pl.pallas_call(
        _inproj_kernel,
        grid=(n_tok // tm,),
        in_specs=[
            pl.BlockSpec((tm, D_MODEL), row),
            pl.BlockSpec((None, 1, D_MODEL), lay3),
            pl.BlockSpec((None, D_MODEL, IN_WIDTH), lay3),
            pl.BlockSpec((None, 1, QK_WIDTH), lay3),
            pl.BlockSpec((None, 1, QK_WIDTH), lay3),
        ],
        out_specs=[pl.BlockSpec((tm, QK_WIDTH), row)] * 8,
        out_shape=[out_sds] * 8,
        compiler_params=_cparams(("parallel",)),
        name="inproj",
    )(x, norm_w, w_in_bf16, lbf, lbb)


FNET_RADIX = 4


@functools.lru_cache(maxsize=None)
def _fnet_tables(seq):
    m = seq // FNET_RADIX
    k = np.arange(m, dtype=np.float64)
    ang = 2.0 * np.pi * np.outer(k, k) / m
    cs = np.concatenate([np.cos(ang), np.sin(ang)], axis=0)
    tw = []
    for r in range(1, FNET_RADIX):
        a = 2.0 * np.pi * r * k / seq
        tw.append(np.broadcast_to(np.cos(a)[:, None], (m, V7X_LANES)))
        tw.append(np.broadcast_to(np.sin(a)[:, None], (m, V7X_LANES)))
    tw = np.stack(tw, axis=0)
    c = np.arange(FNET_GROUP_DIM, dtype=np.float64)
    angc = 2.0 * np.pi * np.outer(c, c) / FNET_GROUP_DIM
    scale = 1.0 / np.sqrt(float(seq) * FNET_GROUP_DIM)
    ch = np.concatenate([np.cos(angc), np.sin(angc)], axis=0) * scale
    return (jnp.asarray(cs, dtype=BF16), jnp.asarray(tw, dtype=F32), jnp.asarray(ch, dtype=BF16))


def _fnet_kernel(x0_ref, x1_ref, x2_ref, x3_ref, cs_ref, tw_ref, ch_ref, w_ref, o_ref):
    m = x0_ref.shape[0]
    x = jnp.concatenate([r[...].astype(BF16) for r in (x0_ref, x1_ref, x2_ref, x3_ref)], axis=1)
    pq = jnp.dot(cs_ref[...], x, preferred_element_type=F32)
    h_re, h_im = [], []
    for r in range(FNET_RADIX):
        p = pq[:m, r * V7X_LANES:(r + 1) * V7X_LANES]
        q = pq[m:, r * V7X_LANES:(r + 1) * V7X_LANES]
        if r == 0:
            h_re.append(p)
            h_im.append(-q)
        else:
            tc = tw_ref[2 * (r - 1)]
            ts = tw_ref[2 * (r - 1) + 1]
            h_re.append(tc * p - ts * q)
            h_im.append(-(tc * q + ts * p))
    x_re = [h_re[0] + h_re[1] + h_re[2] + h_re[3],
            h_re[0] + h_im[1] - h_re[2] - h_im[3],
            h_re[0] - h_re[1] + h_re[2] - h_re[3],
            h_re[0] - h_im[1] - h_re[2] + h_im[3]]
    x_im = [h_im[0] + h_im[1] + h_im[2] + h_im[3],
            h_im[0] - h_re[1] - h_im[2] + h_re[3],
            h_im[0] - h_im[1] + h_im[2] - h_im[3],
            h_im[0] + h_re[1] - h_im[2] - h_re[3]]
    xc = jnp.concatenate([jnp.concatenate([a, b], axis=1) for a, b in zip(x_re, x_im)], axis=0)
    mixed = jnp.dot(xc.astype(BF16), ch_ref[...], preferred_element_type=F32)
    o_ref[...] = jnp.dot(mixed.astype(BF16), w_ref[...], preferred_element_type=F32)


def _fnet(u, w_fnet_bf16, layer, bsz, seq):
    m = seq // FNET_RADIX
    cs, tw, ch = _fnet_tables(seq)
    u4 = u.reshape(bsz, m, FNET_RADIX * FNET_WIDTH)
    x_specs = [pl.BlockSpec((None, m, FNET_GROUP_DIM),
                            functools.partial(lambda b, g, r: (b, 0, r * FNET_GROUPS + g), r=r))
               for r in range(FNET_RADIX)]
    out = pl.pallas_call(
        _fnet_kernel,
        grid=(bsz, FNET_GROUPS),
        in_specs=x_specs + [
            pl.BlockSpec((2 * m, m), lambda b, g: (0, 0)),
            pl.BlockSpec((2 * (FNET_RADIX - 1), m, V7X_LANES), lambda b, g: (0, 0, 0)),
            pl.BlockSpec((2 * FNET_GROUP_DIM, FNET_GROUP_DIM), lambda b, g: (0, 0)),
            pl.BlockSpec((None, None, FNET_GROUP_DIM, FNET_GROUP_DIM), lambda b, g: (layer, g, 0, 0)),
        ],
        out_specs=pl.BlockSpec((None, seq, FNET_GROUP_DIM), lambda b, g: (b, 0, g)),
        out_shape=jax.ShapeDtypeStruct((bsz, seq, FNET_WIDTH), F32),
        compiler_params=_cparams(("parallel", "parallel")),
        name="fnet",
    )(u4, u4, u4, u4, cs, tw, ch, w_fnet_bf16)
    return out.reshape(bsz * seq, FNET_WIDTH)


@functools.lru_cache(maxsize=None)
def _hgrn_tables():
    t = np.arange(CHUNK)
    tri = np.stack([(t[None, :] <= t[:, None]), (t[None, :] >= t[:, None])]).astype(np.float32)
    mask = np.zeros((2, CHUNK, (N_SUB - 1) * CHUNK), np.float32)
    for j in range(N_SUB - 1):
        cols = slice(j * CHUNK, (j + 1) * CHUNK)
        mask[0, :, cols] = ((t[:, None] // SUB) == j + 1) & (t[None, :] < SUB * (j + 1))
        mask[1, :, cols] = ((t[:, None] // SUB) == j) & (t[None, :] >= SUB * (j + 1))
    return jnp.asarray(tri), jnp.asarray(mask)


def _hgrn_intra(c, reverse, q_ref, k_ref, g_ref, v_ref, tri_ref, mask_ref, o_ref, ut_ref, qd_ref, de_ref):
    d = 1 if reverse else 0
    rows = pl.ds(pl.multiple_of(c * CHUNK, CHUNK), CHUNK)
    g = g_ref[rows, :]
    q = q_ref[rows, :]
    k = k_ref[rows, :]
    v = v_ref[rows, :]
    p = jnp.dot(tri_ref[d], g, preferred_element_type=F32, precision=lax.Precision.HIGHEST)
    edge = p[0:1] if reverse else p[CHUNK - 1:CHUNK]
    vb = v.astype(BF16)

    qd_ref[rows, :] = (q * jnp.exp(p)).astype(BF16)
    kdec = (k * jnp.exp(edge - p)).astype(BF16)
    ut_ref[c] = lax.dot_general(vb, kdec, (((0,), (0,)), ((), ())), preferred_element_type=F32)
    de_ref[c] = jnp.broadcast_to(jnp.exp(edge), (V7X_SUBLANES, HGRN_DK))

    zero_ref = jnp.zeros((SUB, HGRN_DK), F32)
    q_refs = []
    for i in range(N_SUB):
        r = SUB * (i + 1) if reverse else SUB * i - 1
        inside = 0 <= r < CHUNK
        q_refs.append(jnp.broadcast_to(p[r:r + 1], (SUB, HGRN_DK)) if inside else zero_ref)
    q1 = (q * jnp.exp(p - jnp.concatenate(q_refs, axis=0))).astype(BF16)
    k_parts = []
    for j in range(N_SUB - 1):
        r = SUB * (j + 1) if reverse else SUB * (j + 1) - 1
        k_parts.append((k * jnp.exp(jnp.minimum(p[r:r + 1] - p, 0.0))).astype(BF16))
    scores = lax.dot_general(q1, jnp.concatenate(k_parts, axis=0), (((1,), (1,)), ((), ())),
                             preferred_element_type=F32)
    w_off = (scores * mask_ref[d]).astype(BF16)
    out = jnp.dot(w_off, jnp.concatenate([vb] * (N_SUB - 1), axis=0), preferred_element_type=F32)

    t_sub = lax.broadcasted_iota(jnp.int32, (CHUNK, 1), 0) % SUB
    out = out + jnp.sum(q * k, axis=-1, keepdims=True) * v
    for delta in range(1, SUB):
        shift = CHUNK - delta if reverse else delta
        k_s = pltpu.roll(k, shift, 0)
        p_s = pltpu.roll(p, shift, 0)
        v_s = pltpu.roll(v, shift, 0)
        valid = (t_sub <= SUB - 1 - delta) if reverse else (t_sub >= delta)
        a = jnp.sum(q * k_s * jnp.exp(jnp.minimum(p - p_s, 0.0)), axis=-1, keepdims=True)
        out = out + jnp.where(valid, a, 0.0) * v_s

    if reverse:
        o_ref[rows, :] = o_ref[rows, :] + out
    else:
        o_ref[rows, :] = out


def _hgrn_inter(i, state, reverse, n_chunks, o_ref, ut_ref, qd_ref, de_ref):
    c = n_chunks - 1 - i if reverse else i
    rows = pl.ds(pl.multiple_of(c * CHUNK, CHUNK), CHUNK)
    o_ref[rows, :] = o_ref[rows, :] + lax.dot_general(
        qd_ref[rows, :], state.astype(BF16), (((1,), (1,)), ((), ())), preferred_element_type=F32)
    return state * de_ref[c][0:1] + ut_ref[c]


def _hgrn_kernel(q_ref, kf_ref, kb_ref, lf_ref, lb_ref, v_ref, gate_ref, nw_ref, tri_ref, mask_ref,
                 y_ref, o_ref, ut_ref, qd_ref, de_ref):
    seq = q_ref.shape[0]
    n_chunks = seq // CHUNK
    for reverse, k_ref, g_ref in ((False, kf_ref, lf_ref), (True, kb_ref, lb_ref)):
        def intra(c, carry, reverse=reverse, k_ref=k_ref, g_ref=g_ref):
            _hgrn_intra(c, reverse, q_ref, k_ref, g_ref, v_ref, tri_ref, mask_ref, o_ref, ut_ref, qd_ref, de_ref)
            return carry
        lax.fori_loop(0, n_chunks, intra, 0)
        lax.fori_loop(0, n_chunks,
                      functools.partial(_hgrn_inter, reverse=reverse, n_chunks=n_chunks, o_ref=o_ref,
                                        ut_ref=ut_ref, qd_ref=qd_ref, de_ref=de_ref),
                      jnp.zeros((HGRN_DV, HGRN_DK), F32))
    o = o_ref[...]
    ms = jnp.mean(o * o, axis=-1, keepdims=True)
    y_ref[...] = o * lax.rsqrt(ms + RMS_EPS) * nw_ref[...] * gate_ref[...]


def _hgrn(q, kf, kb, lf, lb, v, gate, out_norm, layer, bsz, seq):
    tri, mask = _hgrn_tables()
    n_chunks = seq // CHUNK
    as3 = lambda a: a.reshape(bsz, seq, QK_WIDTH)
    head = pl.BlockSpec((None, seq, HGRN_DK), lambda b, h: (b, 0, h))
    out = pl.pallas_call(
        _hgrn_kernel,
        grid=(bsz, HGRN_HEADS),
        in_specs=[head] * 7 + [
            pl.BlockSpec((None, 1, HGRN_DV), lambda b, h: (layer, 0, 0)),
            pl.BlockSpec(tri.shape, lambda b, h: (0, 0, 0)),
            pl.BlockSpec(mask.shape, lambda b, h: (0, 0, 0)),
        ],
        out_specs=head,
        out_shape=jax.ShapeDtypeStruct((bsz, seq, V_WIDTH), F32),
        scratch_shapes=[
            pltpu.VMEM((seq, HGRN_DV), F32),
            pltpu.VMEM((n_chunks, HGRN_DV, HGRN_DK), F32),
            pltpu.VMEM((seq, HGRN_DK), BF16),
            pltpu.VMEM((n_chunks, V7X_SUBLANES, HGRN_DK), F32),
        ],
        compiler_params=_cparams(("parallel", "parallel")),
        name="hgrn",
    )(as3(q), as3(kf), as3(kb), as3(lf), as3(lb), as3(v), as3(gate), out_norm, tri, mask)
    return out.reshape(bsz * seq, V_WIDTH)


def _outproj_kernel(yf_ref, yr_ref, x_ref, w_ref, nw_ref, wr_ref, x2_ref, h2_ref, aff_ref):
    x2 = (x_ref[...]
          + jnp.dot(yf_ref[...].astype(BF16), w_ref[:FNET_WIDTH, :], preferred_element_type=F32)
          + jnp.dot(yr_ref[...].astype(BF16), w_ref[FNET_WIDTH:, :], preferred_element_type=F32))
    x2_ref[...] = x2
    ms = jnp.mean(x2 * x2, axis=-1, keepdims=True)
    h2 = x2 * lax.rsqrt(ms + RMS_EPS) * nw_ref[...]
    h2_ref[...] = h2
    logits = lax.dot_general(wr_ref[...], h2, (((1,), (1,)), ((), ())), preferred_element_type=F32,
                             precision=lax.Precision.HIGHEST)
    e = jnp.exp(logits - jnp.max(logits, axis=0, keepdims=True))
    aff = e / jnp.sum(e, axis=0, keepdims=True)
    for j in range(aff_ref.shape[0]):
        aff_ref[j] = aff[:, j * ROUTE_CHUNK:(j + 1) * ROUTE_CHUNK]


def _outproj(yf, yr, x, w_out_bf16, norm_w, w_router_t, layer):
    n_tok = x.shape[0]
    tm = min(ROW_TILE, n_tok)
    row = lambda i: (i, 0)
    lay3 = lambda i: (layer, 0, 0)
    return pl.pallas_call(
        _outproj_kernel,
        grid=(n_tok // tm,),
        in_specs=[
            pl.BlockSpec((tm, FNET_WIDTH), row),
            pl.BlockSpec((tm, V_WIDTH), row),
            pl.BlockSpec((tm, D_MODEL), row),
            pl.BlockSpec((None, D_MODEL, D_MODEL), lay3),
            pl.BlockSpec((None, 1, D_MODEL), lay3),
            pl.BlockSpec((None, N_EXPERTS, D_MODEL), lay3),
        ],
        out_specs=[
            pl.BlockSpec((tm, D_MODEL), row),
            pl.BlockSpec((tm, D_MODEL), row),
            pl.BlockSpec((tm // ROUTE_CHUNK, N_EXPERTS, ROUTE_CHUNK), lambda i: (i, 0, 0)),
        ],
        out_shape=[
            jax.ShapeDtypeStruct((n_tok, D_MODEL), F32),
            jax.ShapeDtypeStruct((n_tok, D_MODEL), F32),
            jax.ShapeDtypeStruct((n_tok // ROUTE_CHUNK, N_EXPERTS, ROUTE_CHUNK), F32),
        ],
        compiler_params=_cparams(("parallel",)),
        name="outproj",
    )(yf, yr, x, w_out_bf16, norm_w, w_router_t)


ONE_F32_BITS = 0x3F800000
BISECT_STEPS = 31


def _select_kernel(aff_ref, tri_ref, slot_ref, off_ref, *, cap):
    n_chunks = aff_ref.shape[0]
    bits = pltpu.bitcast(aff_ref[...], jnp.int32)

    def count(pred):
        return jnp.sum(jnp.sum(pred.astype(F32), axis=0), axis=1, keepdims=True)

    def bisect(_, lo_hi):
        lo, hi = lo_hi
        mid = lo + ((hi - lo + 1) >> 1)
        ok = count(bits >= mid[None]) >= cap
        return jnp.where(ok, mid, lo), jnp.where(ok, hi, mid - 1)

    cut, _ = lax.fori_loop(0, BISECT_STEPS, bisect,
                           (jnp.zeros((N_EXPERTS, 1), jnp.int32),
                            jnp.full((N_EXPERTS, 1), ONE_F32_BITS, jnp.int32)))
    ties_wanted = cap - count(bits > cut[None])

    def scan(c, carry):
        n_sel, n_tie = carry
        b = pltpu.bitcast(aff_ref[c], jnp.int32)
        above = b > cut
        tie = (b == cut).astype(F32)
        tie_rank = n_tie + jnp.dot(tie.astype(BF16), tri_ref[...], preferred_element_type=F32) - tie
        sel = jnp.where(above, 1.0, jnp.where(tie_rank < ties_wanted, tie, 0.0))
        slot = n_sel + jnp.dot(sel.astype(BF16), tri_ref[...], preferred_element_type=F32) - sel
        slot_ref[c] = jnp.where(sel > 0.0, slot, -1.0).astype(jnp.int32)
        off_ref[c] = jnp.broadcast_to(n_sel, (N_EXPERTS, ROUTE_CHUNK)).astype(jnp.int32)
        return (n_sel + jnp.sum(sel, axis=1, keepdims=True), n_tie + jnp.sum(tie, axis=1, keepdims=True))

    zero = jnp.zeros((N_EXPERTS, 1), F32)
    lax.fori_loop(0, n_chunks, scan, (zero, zero))


@functools.lru_cache(maxsize=None)
def _prefix_table():
    t = np.arange(ROUTE_CHUNK)
    return jnp.asarray((t[:, None] <= t[None, :]).astype(np.float32), dtype=BF16)


def _select(aff, cap):
    n_chunks = aff.shape[0]
    sds = jax.ShapeDtypeStruct((n_chunks, N_EXPERTS, ROUTE_CHUNK), jnp.int32)
    return pl.pallas_call(
        functools.partial(_select_kernel, cap=cap),
        out_shape=[sds, sds],
        compiler_params=pltpu.CompilerParams(vmem_limit_bytes=V7X_VMEM_LIMIT_BYTES),
        name="route_select",
    )(aff, _prefix_table())


PLACE_ROWS = 8


def _place_kernel(off_sm, slot_ref, aff_ref, idx_ref, gate_ref, acc_ref):
    n_chunks = slot_ref.shape[0]
    n_tiles = acc_ref.shape[1]
    acc_ref[...] = jnp.zeros_like(acc_ref)
    lane = lax.broadcasted_iota(jnp.int32, (1, ROUTE_CHUNK), 1)
    j_iota = lax.broadcasted_iota(jnp.int32, (ROUTE_CHUNK, ROUTE_CHUNK), 0)
    lane8 = lax.broadcasted_iota(jnp.int32, (PLACE_ROWS, ROUTE_CHUNK), 1)
    lo_row = lane.astype(F32)

    def chunk(c, carry):
        slots = slot_ref[c]
        aff = aff_ref[c]
        hi_row = jnp.full((1, ROUTE_CHUNK), c, jnp.int32).astype(F32)
        for e in range(N_EXPERTS):
            off = off_sm[c * N_EXPERTS + e]
            tile = off >> 7
            shift = off & (ROUTE_CHUNK - 1)
            local = slots[e:e + 1] - off
            onehot = jnp.where(local == j_iota, 1.0, 0.0).astype(BF16)
            a = aff[e:e + 1]
            g1 = a.astype(BF16)
            g2 = (a - g1.astype(F32)).astype(BF16)
            g3 = (a - g1.astype(F32) - g2.astype(F32)).astype(BF16)
            lhs = jnp.concatenate([hi_row.astype(BF16), lo_row.astype(BF16), g1, g2, g3,
                                   jnp.zeros((PLACE_ROWS - 5, ROUTE_CHUNK), BF16)], axis=0)
            packed = lax.dot_general(lhs, onehot, (((1,), (1,)), ((), ())), preferred_element_type=F32)
            rolled = pltpu.roll(packed, shift, 1)
            acc_ref[e, tile] = acc_ref[e, tile] + jnp.where(lane8 >= shift, rolled, 0.0)
            acc_ref[e, tile + 1] = acc_ref[e, tile + 1] + jnp.where(lane8 < shift, rolled, 0.0)
        return carry

    lax.fori_loop(0, n_chunks, chunk, 0)
    for e in range(N_EXPERTS):
        hi = acc_ref[e, :, 0, :]
        lo = acc_ref[e, :, 1, :]
        idx_ref[e] = (hi * float(ROUTE_CHUNK) + lo).astype(jnp.int32)[:n_tiles - 1]
        gate_ref[e] = (acc_ref[e, :, 2, :] + acc_ref[e, :, 3, :] + acc_ref[e, :, 4, :])[:n_tiles - 1]


def _place(offsets_flat, slots, aff, cap):
    n_tiles = cap // ROUTE_CHUNK
    return pl.pallas_call(
        _place_kernel,
        grid_spec=pltpu.PrefetchScalarGridSpec(
            num_scalar_prefetch=1,
            grid=(1,),
            in_specs=[pl.BlockSpec(slots.shape, lambda i, off: (0, 0, 0)),
                      pl.BlockSpec(aff.shape, lambda i, off: (0, 0, 0))],
            out_specs=[pl.BlockSpec((N_EXPERTS, n_tiles, ROUTE_CHUNK), lambda i, off: (0, 0, 0))] * 2,
            scratch_shapes=[pltpu.VMEM((N_EXPERTS, n_tiles + 1, PLACE_ROWS, ROUTE_CHUNK), F32)],
        ),
        out_shape=[jax.ShapeDtypeStruct((N_EXPERTS, n_tiles, ROUTE_CHUNK), jnp.int32),
                   jax.ShapeDtypeStruct((N_EXPERTS, n_tiles, ROUTE_CHUNK), F32)],
        compiler_params=_cparams(("arbitrary",)),
        name="route_place",
    )(offsets_flat, slots, aff)


def _ffn_kernel(idx_sm, h_hbm, gate_ref, wg_ref, wu_ref, wd_ref, y_ref, xbuf, sem):
    n_k = pl.num_programs(1)
    step = pl.program_id(0) * n_k + pl.program_id(1)
    n_steps = pl.num_programs(0) * n_k
    slot = step % 2

    def row_copy(tile, buf, j):
        row = idx_sm[tile * SLOT_TILE + j]
        return pltpu.make_async_copy(h_hbm.at[pl.ds(row, 1)], xbuf.at[buf, pl.ds(j, 1)], sem.at[buf])

    def gather(tile, buf):
        def body(j, carry):
            row_copy(tile, buf, j).start()
            return carry
        lax.fori_loop(0, SLOT_TILE, body, 0)

    @pl.when(step == 0)
    def _():
        gather(0, 0)

    @pl.when(step + 1 < n_steps)
    def _():
        gather(step + 1, 1 - slot)

    pltpu.make_async_copy(h_hbm.at[pl.ds(0, SLOT_TILE)], xbuf.at[slot], sem.at[slot]).wait()

    x = xbuf[slot].astype(BF16)
    hg = jnp.dot(x, wg_ref[...], preferred_element_type=F32)
    hu = jnp.dot(x, wu_ref[...], preferred_element_type=F32)
    he = (hg * _sigmoid(hg) * hu).astype(BF16)
    y_ref[...] = jnp.dot(he, wd_ref[...], preferred_element_type=F32) * gate_ref[...]


def _ffn(idx_flat, h, gate_col, wg_bf16, wu_bf16, wd_bf16, layer, cap):
    n_k = cap // SLOT_TILE
    wspec = lambda shape: pl.BlockSpec((None, None) + shape, lambda e, k, idx: (layer, e, 0, 0))
    tile = lambda e, k, idx: (e * n_k + k, 0)
    return pl.pallas_call(
        _ffn_kernel,
        grid_spec=pltpu.PrefetchScalarGridSpec(
            num_scalar_prefetch=1,
            grid=(N_EXPERTS, n_k),
            in_specs=[pl.BlockSpec(memory_space=pl.ANY),
                      pl.BlockSpec((SLOT_TILE, 1), tile),
                      wspec((D_MODEL, D_FF_EXPERT)), wspec((D_MODEL, D_FF_EXPERT)),
                      wspec((D_FF_EXPERT, D_MODEL))],
            out_specs=pl.BlockSpec((SLOT_TILE, D_MODEL), tile),
            scratch_shapes=[pltpu.VMEM((2, SLOT_TILE, D_MODEL), F32), pltpu.SemaphoreType.DMA((2,))],
        ),
        out_shape=jax.ShapeDtypeStruct((N_EXPERTS * cap, D_MODEL), F32),
        compiler_params=_cparams(("arbitrary", "arbitrary")),
        name="expert_ffn",
    )(idx_flat, h, gate_col, wg_bf16, wu_bf16, wd_bf16)


def _combine_kernel(off_sm, x_ref, ya_ref, yb_ref, ia_ref, ib_ref, o_ref, *, cap):
    i = pl.program_id(0)
    e = pl.program_id(1)
    n_k = cap // SLOT_TILE
    chunks_per_tile = TOKEN_TILE // ROUTE_CHUNK
    s0 = off_sm[i * chunks_per_tile * N_EXPERTS + e]
    s1 = off_sm[(i + 1) * chunks_per_tile * N_EXPERTS + e]
    blk = e * n_k + s0 // SLOT_TILE

    @pl.when(e == 0)
    def _():
        o_ref[...] = x_ref[...]

    t_iota = lax.broadcasted_iota(jnp.int32, (TOKEN_TILE, SLOT_TILE), 0)
    lane = lax.broadcasted_iota(jnp.int32, (1, SLOT_TILE), 1)

    def expand(y_ref, idx_ref, block):
        slot = block * SLOT_TILE + lane
        valid = (slot >= e * cap + s0) & (slot < e * cap + s1)
        t_local = jnp.where(valid, idx_ref[...] - i * TOKEN_TILE, -1)
        onehot = jnp.where(t_local == t_iota, 1.0, 0.0).astype(BF16)
        y = y_ref[...]
        y_hi = y.astype(BF16)
        y_lo = (y - y_hi.astype(F32)).astype(BF16)
        o_ref[...] = (o_ref[...] + jnp.dot(onehot, y_hi, preferred_element_type=F32)
                      + jnp.dot(onehot, y_lo, preferred_element_type=F32))

    @pl.when(s1 > s0)
    def _():
        expand(ya_ref, ia_ref, blk)

    @pl.when(e * cap + s1 > (blk + 1) * SLOT_TILE)
    def _():
        expand(yb_ref, ib_ref, blk + 1)


def _combine(offsets_flat, x, ye, idx_blocks, cap):
    n_tok = x.shape[0]
    n_k = cap // SLOT_TILE
    last = N_EXPERTS * n_k - 1
    chunks_per_tile = TOKEN_TILE // ROUTE_CHUNK

    def block_a(i, e, off):
        return jnp.minimum(e * n_k + off[i * chunks_per_tile * N_EXPERTS + e] // SLOT_TILE, last)

    def block_b(i, e, off):
        return jnp.minimum(e * n_k + off[i * chunks_per_tile * N_EXPERTS + e] // SLOT_TILE + 1, last)

    tok = pl.BlockSpec((TOKEN_TILE, D_MODEL), lambda i, e, off: (i, 0))
    return pl.pallas_call(
        functools.partial(_combine_kernel, cap=cap),
        grid_spec=pltpu.PrefetchScalarGridSpec(
            num_scalar_prefetch=1,
            grid=(n_tok // TOKEN_TILE, N_EXPERTS),
            in_specs=[tok,
                      pl.BlockSpec((SLOT_TILE, D_MODEL), lambda i, e, off: (block_a(i, e, off), 0)),
                      pl.BlockSpec((SLOT_TILE, D_MODEL), lambda i, e, off: (block_b(i, e, off), 0)),
                      pl.BlockSpec((None, 1, SLOT_TILE), lambda i, e, off: (block_a(i, e, off), 0, 0)),
                      pl.BlockSpec((None, 1, SLOT_TILE), lambda i, e, off: (block_b(i, e, off), 0, 0))],
            out_specs=tok,
        ),
        out_shape=jax.ShapeDtypeStruct((n_tok, D_MODEL), F32),
        compiler_params=_cparams(("parallel", "arbitrary")),
        name="combine",
    )(offsets_flat, x, ye, ye, idx_blocks, idx_blocks)


def _final_norm_kernel(x_ref, nw_ref, o_ref):
    x = x_ref[...]
    ms = jnp.mean(x * x, axis=-1, keepdims=True)
    o_ref[...] = x * lax.rsqrt(ms + RMS_EPS) * nw_ref[...]


def _final_norm(x, norm_w):
    n_tok = x.shape[0]
    tm = min(ROW_TILE, n_tok)
    return pl.pallas_call(
        _final_norm_kernel,
        grid=(n_tok // tm,),
        in_specs=[pl.BlockSpec((tm, D_MODEL), lambda i: (i, 0)), pl.BlockSpec((1, D_MODEL), lambda i: (0, 0))],
        out_specs=pl.BlockSpec((tm, D_MODEL), lambda i: (i, 0)),
        out_shape=jax.ShapeDtypeStruct((n_tok, D_MODEL), F32),
        compiler_params=_cparams(("parallel",)),
        name="final_norm",
    )(x, norm_w)


def _moe(x2, h2, aff, params, layer):
    n_tok = x2.shape[0]
    cap = EC_CAPACITY_FACTOR * n_tok // N_EXPERTS
    slots, offs = _select(aff, cap)
    offsets_flat = jnp.concatenate(
        [offs[:, :, 0], jnp.full((1, N_EXPERTS), cap, jnp.int32)], axis=0).reshape(-1)
    idx, gate = _place(offsets_flat, slots, aff, cap)
    ye = _ffn(idx.reshape(-1), h2, gate.reshape(-1, 1), params["w_gate"], params["w_up"], params["w_down"],
              layer, cap)
    return _combine(offsets_flat, x2, ye, idx.reshape(-1, 1, SLOT_TILE), cap)


def _trunk(x, params):
    bsz, seq, _ = x.shape
    x = x.reshape(bsz * seq, D_MODEL)
    for layer in range(DEPTH):
        u, q, kf, kb, lf, lb, v, g = _inproj(x, params["norm_mix"], params["w_in"], params["lbf"], params["lbb"], layer)
        yf = _fnet(u, params["w_fnet"], layer, bsz, seq)
        yr = _hgrn(q, kf, kb, lf, lb, v, g, params["hgrn_out_norm"], layer, bsz, seq)
        x2, h2, aff = _outproj(yf, yr, x, params["w_out"], params["norm_ffn"], params["w_router_t"], layer)
        x = _moe(x2, h2, aff, params, layer)
    return _final_norm(x, params["norm_final"]).reshape(bsz, seq, D_MODEL)


def kernel(x_prompt, x_sample, norm_mix, w_in, w_fnet, lb_fwd, lb_bwd, hgrn_out_norm, w_out,
           norm_ffn, w_router, w_gate, w_up, w_down, norm_final):
    params = {
        "norm_mix": norm_mix.reshape(DEPTH, 1, D_MODEL),
        "w_in": w_in.astype(BF16),
        "w_fnet": w_fnet.astype(BF16),
        "lbf": _lower_bounds(lb_fwd).reshape(DEPTH, 1, QK_WIDTH),
        "lbb": _lower_bounds(lb_bwd).reshape(DEPTH, 1, QK_WIDTH),
        "hgrn_out_norm": hgrn_out_norm.reshape(DEPTH, 1, HGRN_DV),
        "w_out": w_out.astype(BF16),
        "norm_ffn": norm_ffn.reshape(DEPTH, 1, D_MODEL),
        "w_router_t": jnp.swapaxes(w_router, 1, 2),
        "w_gate": w_gate.astype(BF16),
        "w_up": w_up.astype(BF16),
        "w_down": w_down.astype(BF16),
        "norm_final": norm_final.reshape(1, D_MODEL),
    }
    return _trunk(x_prompt, params), _trunk(x_sample, params)
```

```python
import functools

import numpy as np
import jax
import jax.numpy as jnp
from jax import lax
from jax.experimental import pallas as pl
from jax.experimental.pallas import tpu as pltpu

D_MODEL = 1024
DEPTH = 4
FNET_WIDTH = 512
FNET_GROUPS = 4
FNET_GROUP_DIM = 128
HGRN_HEADS = 4
HGRN_DK = 128
HGRN_DV = 128
QK_WIDTH = 512
V_WIDTH = 512
IN_WIDTH = FNET_WIDTH + 3 * QK_WIDTH + 2 * V_WIDTH
N_EXPERTS = 16
EC_CAPACITY_FACTOR = 2
D_FF_EXPERT = 2048
RMS_EPS = 1e-6

V7X_LANES = 128
V7X_SUBLANES = 8
V7X_VMEM_LIMIT_BYTES = 56 * 1024 * 1024

BF16 = jnp.bfloat16
F32 = jnp.float32

CHUNK = 64

ROW_TILE = 512
SLOT_TILE = 256
ROUTE_CHUNK = 128


def _cparams(sem, vmem=V7X_VMEM_LIMIT_BYTES):
    return pltpu.CompilerParams(dimension_semantics=sem, vmem_limit_bytes=vmem)


def _sigmoid(x):
    return 1.0 / (1.0 + jnp.exp(-x))


def _lower_bounds_kernel(p_ref, o_ref):
    p = p_ref[...]
    m = jnp.max(p, axis=0, keepdims=True)
    e = jnp.exp(p - m)
    sm = e / jnp.sum(e, axis=0, keepdims=True)
    acc = None
    rows = []
    for layer in range(DEPTH):
        acc = sm[layer:layer + 1] if acc is None else acc + sm[layer:layer + 1]
        rows.append(jnp.maximum(acc - sm[0:1], 0.0))
    o_ref[...] = jnp.concatenate(rows, axis=0)


def _lower_bounds(lb_param):
    return pl.pallas_call(
        _lower_bounds_kernel,
        out_shape=jax.ShapeDtypeStruct((DEPTH, QK_WIDTH), F32),
        name="lower_bounds",
    )(lb_param)


def _hgrn_gates(z, lb):
    e = jnp.exp(-jnp.abs(z))
    r = 1.0 / (1.0 + e)
    sig_neg = jnp.where(z >= 0, e * r, r)
    k = (1.0 - lb) * sig_neg
    log_sig = jnp.minimum(z, 0.0) - jnp.log1p(e)
    c = jnp.log1p(-lb) + log_sig
    has_lb = lb > 0.0
    a = jnp.log(jnp.where(has_lb, lb, 1.0))
    logf = jnp.where(has_lb, jnp.maximum(a, c) + jnp.log1p(jnp.exp(-jnp.abs(a - c))), c)
    return k, logf


def _inproj_kernel(x_ref, nw_ref, w_ref, lbf_ref, lbb_ref,
                   u_ref, q_ref, kf_ref, kb_ref, lf_ref, lbo_ref, v_ref, g_ref):
    x = x_ref[...]
    ms = jnp.mean(x * x, axis=-1, keepdims=True)
    h = (x * lax.rsqrt(ms + RMS_EPS) * nw_ref[...]).astype(BF16)

    def proj(col0, width):
        return jnp.dot(h, w_ref[:, col0:col0 + width], preferred_element_type=F32)

    u_ref[...] = proj(0, FNET_WIDTH)
    q = proj(FNET_WIDTH, QK_WIDTH)
    q_ref[...] = q * _sigmoid(q)
    kf, lf = _hgrn_gates(proj(FNET_WIDTH + QK_WIDTH, QK_WIDTH), lbf_ref[...])
    kf_ref[...] = kf
    lf_ref[...] = lf
    kb, lb = _hgrn_gates(proj(FNET_WIDTH + 2 * QK_WIDTH, QK_WIDTH), lbb_ref[...])
    kb_ref[...] = kb
    lbo_ref[...] = lb
    v_ref[...] = proj(FNET_WIDTH + 3 * QK_WIDTH, V_WIDTH)
    g = proj(FNET_WIDTH + 3 * QK_WIDTH + V_WIDTH, V_WIDTH)
    g_ref[...] = g * _sigmoid(g)


def _inproj(x, norm_w, w_in_bf16, lbf, lbb, layer):
    n_tok = x.shape[0]
    tm = min(ROW_TILE, n_tok)
    row = lambda i: (i, 0)
    lay3 = lambda i: (layer, 0, 0)
    out_sds = jax.ShapeDtypeStruct((n_tok, QK_WIDTH), F32)
    return pl.pallas_call(
        _inproj_kernel,
        grid=(n_tok // tm,),
        in_specs=[
            pl.BlockSpec((tm, D_MODEL), row),
            pl.BlockSpec((None, 1, D_MODEL), lay3),
            pl.BlockSpec((None, D_MODEL, IN_WIDTH), lay3),
            pl.BlockSpec((None, 1, QK_WIDTH), lay3),
            pl.BlockSpec((None, 1, QK_WIDTH), lay3),
        ],
        out_specs=[pl.BlockSpec((tm, QK_WIDTH), row)] * 8,
        out_shape=[out_sds] * 8,
        compiler_params=_cparams(("parallel",)),
        name="inproj",
    )(x, norm_w, w_in_bf16, lbf, lbb)


FNET_RADIX = 4


@functools.lru_cache(maxsize=None)
def _fnet_tables(seq):
    m = seq // FNET_RADIX
    k = np.arange(m, dtype=np.float64)
    ang = 2.0 * np.pi * np.outer(k, k) / m
    cs = np.concatenate([np.cos(ang), np.sin(ang)], axis=0)
    tw = []
    for r in range(1, FNET_RADIX):
        a = 2.0 * np.pi * r * k / seq
        tw.append(np.broadcast_to(np.cos(a)[:, None], (m, V7X_LANES)))
        tw.append(np.broadcast_to(np.sin(a)[:, None], (m, V7X_LANES)))
    tw = np.stack(tw, axis=0)
    c = np.arange(FNET_GROUP_DIM, dtype=np.float64)
    angc = 2.0 * np.pi * np.outer(c, c) / FNET_GROUP_DIM
    scale = 1.0 / np.sqrt(float(seq) * FNET_GROUP_DIM)
    ch = np.concatenate([np.cos(angc), np.sin(angc)], axis=0) * scale
    return (jnp.asarray(cs, dtype=BF16), jnp.asarray(tw, dtype=F32), jnp.asarray(ch, dtype=BF16))


def _fnet_kernel(x0_ref, x1_ref, x2_ref, x3_ref, cs_ref, tw_ref, ch_ref, w_ref, o_ref):
    m = x0_ref.shape[0]
    x = jnp.concatenate([r[...].astype(BF16) for r in (x0_ref, x1_ref, x2_ref, x3_ref)], axis=1)
    pq = jnp.dot(cs_ref[...], x, preferred_element_type=F32)
    h_re, h_im = [], []
    for r in range(FNET_RADIX):
        p = pq[:m, r * V7X_LANES:(r + 1) * V7X_LANES]
        q = pq[m:, r * V7X_LANES:(r + 1) * V7X_LANES]
        if r == 0:
            h_re.append(p)
            h_im.append(-q)
        else:
            tc = tw_ref[2 * (r - 1)]
            ts = tw_ref[2 * (r - 1) + 1]
            h_re.append(tc * p - ts * q)
            h_im.append(-(tc * q + ts * p))
    x_re = [h_re[0] + h_re[1] + h_re[2] + h_re[3],
            h_re[0] + h_im[1] - h_re[2] - h_im[3],
            h_re[0] - h_re[1] + h_re[2] - h_re[3],
            h_re[0] - h_im[1] - h_re[2] + h_im[3]]
    x_im = [h_im[0] + h_im[1] + h_im[2] + h_im[3],
            h_im[0] - h_re[1] - h_im[2] + h_re[3],
            h_im[0] - h_im[1] + h_im[2] - h_im[3],
            h_im[0] + h_re[1] - h_im[2] - h_re[3]]
    xc = jnp.concatenate([jnp.concatenate([a, b], axis=1) for a, b in zip(x_re, x_im)], axis=0)
    mixed = jnp.dot(xc.astype(BF16), ch_ref[...], preferred_element_type=F32)
    o_ref[...] = jnp.dot(mixed.astype(BF16), w_ref[...], preferred_element_type=F32)


def _fnet(u, w_fnet_bf16, layer, bsz, seq):
    m = seq // FNET_RADIX
    cs, tw, ch = _fnet_tables(seq)
    u4 = u.reshape(bsz, m, FNET_RADIX * FNET_WIDTH)
    x_specs = [pl.BlockSpec((None, m, FNET_GROUP_DIM),
                            functools.partial(lambda b, g, r: (b, 0, r * FNET_GROUPS + g), r=r))
               for r in range(FNET_RADIX)]
    out = pl.pallas_call(
        _fnet_kernel,
        grid=(bsz, FNET_GROUPS),
        in_specs=x_specs + [
            pl.BlockSpec((2 * m, m), lambda b, g: (0, 0)),
            pl.BlockSpec((2 * (FNET_RADIX - 1), m, V7X_LANES), lambda b, g: (0, 0, 0)),
            pl.BlockSpec((2 * FNET_GROUP_DIM, FNET_GROUP_DIM), lambda b, g: (0, 0)),
            pl.BlockSpec((None, None, FNET_GROUP_DIM, FNET_GROUP_DIM), lambda b, g: (layer, g, 0, 0)),
        ],
        out_specs=pl.BlockSpec((None, seq, FNET_GROUP_DIM), lambda b, g: (b, 0, g)),
        out_shape=jax.ShapeDtypeStruct((bsz, seq, FNET_WIDTH), F32),
        compiler_params=_cparams(("parallel", "parallel")),
        name="fnet",
    )(u4, u4, u4, u4, cs, tw, ch, w_fnet_bf16)
    return out.reshape(bsz * seq, FNET_WIDTH)


PAIR_LEVELS = ((CHUNK, 16), (16, 4))
PAIR_PARTS = 3
BAND = 4
CHUNK_UNROLL = 4


@functools.lru_cache(maxsize=None)
def _hgrn_tables():
    t = np.arange(CHUNK)
    tri = np.stack([(t[None, :] <= t[:, None]), (t[None, :] >= t[:, None])]).astype(np.float32)
    mask = np.zeros((2, len(PAIR_LEVELS), CHUNK, PAIR_PARTS * CHUNK), np.float32)
    for lvl, (outer, inner) in enumerate(PAIR_LEVELS):
        same = (t[:, None] // outer) == (t[None, :] // outer)
        t_in = (t[:, None] % outer) // inner
        s_in = t[None, :] % outer
        for j in range(PAIR_PARTS):
            cols = slice(j * CHUNK, (j + 1) * CHUNK)
            mask[0, lvl, :, cols] = same & (t_in == j + 1) & (s_in < inner * (j + 1))
            mask[1, lvl, :, cols] = same & (t_in == j) & (s_in >= inner * (j + 1))
    return jnp.asarray(tri), jnp.asarray(mask)


def _rows_block(p, rows, block):
    def rep(r, n):
        if 0 <= r < CHUNK:
            return jnp.broadcast_to(p[r:r + 1], (n, HGRN_DK))
        return jnp.zeros((n, HGRN_DK), F32)

    if block >= V7X_SUBLANES:
        return jnp.concatenate([rep(r, block) for r in rows], axis=0)
    upper = lax.broadcasted_iota(jnp.int32, (V7X_SUBLANES, HGRN_DK), 0) < block
    return jnp.concatenate([jnp.where(upper, rep(rows[2 * i], V7X_SUBLANES), rep(rows[2 * i + 1], V7X_SUBLANES))
                            for i in range(len(rows) // 2)], axis=0)


def _level_operands(q, k, p, reverse, outer, inner):
    q_rows = [inner * (ib + 1) if reverse else inner * ib - 1 for ib in range(CHUNK // inner)]
    q_l = (q * jnp.exp(p - _rows_block(p, q_rows, inner))).astype(BF16)
    parts = []
    for j in range(PAIR_PARTS):
        k_rows = [outer * ob + inner * (j + 1) - (0 if reverse else 1) for ob in range(CHUNK // outer)]
        parts.append((k * jnp.exp(jnp.minimum(_rows_block(p, k_rows, outer) - p, 0.0))).astype(BF16))
    return q_l, jnp.concatenate(parts, axis=0)


def _chunk_rows(c):
    return pl.ds(pl.multiple_of(c * CHUNK, CHUNK), CHUNK)


def _hgrn_prepare(cs, reverse, k_ref, g_ref, v_ref, tri_ref, p_ref, ut_ref, de_ref):
    rows = [_chunk_rows(c) for c in cs]
    g_all = jnp.concatenate([g_ref[r, :] for r in rows], axis=1)
    p_all = jnp.dot(tri_ref[1 if reverse else 0], g_all, preferred_element_type=F32,
                    precision=lax.Precision.HIGHEST)
    ps = [p_all[:, i * HGRN_DK:(i + 1) * HGRN_DK] for i in range(len(cs))]
    edges = [p[0:1] if reverse else p[CHUNK - 1:CHUNK] for p in ps]
    kdecs = [(k_ref[r, :] * jnp.exp(e - p)).astype(BF16) for r, e, p in zip(rows, edges, ps)]
    uts = [lax.dot_general(v_ref[r, :].astype(BF16), kd, (((0,), (0,)), ((), ())), preferred_element_type=F32)
           for r, kd in zip(rows, kdecs)]
    for c, r, p, e, ut in zip(cs, rows, ps, edges, uts):
        p_ref[r, :] = p
        ut_ref[c] = ut
        de_ref[c] = jnp.broadcast_to(jnp.exp(e), (V7X_SUBLANES, HGRN_DK))


def _hgrn_scan(i, state, reverse, n_chunks, st_ref, ut_ref, de_ref):
    c = n_chunks - 1 - i if reverse else i
    st_ref[c] = state.astype(BF16)
    return state * de_ref[c][0:1] + ut_ref[c]


def _hgrn_output(cs, reverse, q_ref, k_ref, v_ref, p_ref, st_ref, mask_ref, o_ref):
    d = 1 if reverse else 0
    n = len(cs)
    nt = (((1,), (1,)), ((), ()))
    rows = [_chunk_rows(c) for c in cs]
    qs = [q_ref[r, :] for r in rows]
    ks = [k_ref[r, :] for r in rows]
    vs = [v_ref[r, :] for r in rows]
    ps = [p_ref[r, :] for r in rows]
    v_parts = [jnp.concatenate([v.astype(BF16)] * PAIR_PARTS, axis=0) for v in vs]
    outs = [lax.dot_general((qs[i] * jnp.exp(ps[i])).astype(BF16), st_ref[cs[i]], nt, preferred_element_type=F32)
            for i in range(n)]
    for lvl, (outer, inner) in enumerate(PAIR_LEVELS):
        operands = [_level_operands(qs[i], ks[i], ps[i], reverse, outer, inner) for i in range(n)]
        scores = [lax.dot_general(q_l, k_l, nt, preferred_element_type=F32) for q_l, k_l in operands]
        weights = [(s * mask_ref[d, lvl]).astype(BF16) for s in scores]
        outs = [outs[i] + jnp.dot(weights[i], v_parts[i], preferred_element_type=F32) for i in range(n)]

    t_in = lax.broadcasted_iota(jnp.int32, (CHUNK, 1), 0) % BAND
    for i in range(n):
        q, k, v, p = qs[i], ks[i], vs[i], ps[i]
        out = outs[i] + jnp.sum(q * k, axis=-1, keepdims=True) * v
        for delta in range(1, BAND):
            shift = CHUNK - delta if reverse else delta
            k_s = pltpu.roll(k, shift, 0)
            p_s = pltpu.roll(p, shift, 0)
            v_s = pltpu.roll(v, shift, 0)
            valid = (t_in <= BAND - 1 - delta) if reverse else (t_in >= delta)
            a = jnp.sum(q * k_s * jnp.exp(jnp.minimum(p - p_s, 0.0)), axis=-1, keepdims=True)
            out = out + jnp.where(valid, a, 0.0) * v_s
        if reverse:
            o_ref[rows[i], :] = o_ref[rows[i], :] + out
        else:
            o_ref[rows[i], :] = out


def _hgrn_kernel(q_ref, kf_ref, kb_ref, lf_ref, lb_ref, v_ref, gate_ref, nw_ref, tri_ref, mask_ref,
                 y_ref, o_ref, p_ref, ut_ref, st_ref, de_ref):
    seq = q_ref.shape[0]
    n_chunks = seq // CHUNK

    def per_chunk(fn):
        def body(i, carry):
            fn([i * CHUNK_UNROLL + u for u in range(CHUNK_UNROLL)])
            return carry
        lax.fori_loop(0, n_chunks // CHUNK_UNROLL, body, 0)

    for reverse, k_ref, g_ref in ((False, kf_ref, lf_ref), (True, kb_ref, lb_ref)):
        per_chunk(functools.partial(_hgrn_prepare, reverse=reverse, k_ref=k_ref, g_ref=g_ref, v_ref=v_ref,
                                    tri_ref=tri_ref, p_ref=p_ref, ut_ref=ut_ref, de_ref=de_ref))
        lax.fori_loop(0, n_chunks,
                      functools.partial(_hgrn_scan, reverse=reverse, n_chunks=n_chunks, st_ref=st_ref,
                                        ut_ref=ut_ref, de_ref=de_ref),
                      jnp.zeros((HGRN_DV, HGRN_DK), F32))
        per_chunk(functools.partial(_hgrn_output, reverse=reverse, q_ref=q_ref, k_ref=k_ref, v_ref=v_ref,
                                    p_ref=p_ref, st_ref=st_ref, mask_ref=mask_ref, o_ref=o_ref))
    o = o_ref[...]
    ms = jnp.mean(o * o, axis=-1, keepdims=True)
    y_ref[...] = o * lax.rsqrt(ms + RMS_EPS) * nw_ref[...] * gate_ref[...]


def _hgrn(q, kf, kb, lf, lb, v, gate, out_norm, layer, bsz, seq):
    tri, mask = _hgrn_tables()
    n_chunks = seq // CHUNK
    as3 = lambda a: a.reshape(bsz, seq, QK_WIDTH)
    head = pl.BlockSpec((None, seq, HGRN_DK), lambda b, h: (b, 0, h))
    out = pl.pallas_call(
        _hgrn_kernel,
        grid=(bsz, HGRN_HEADS),
        in_specs=[head] * 7 + [
            pl.BlockSpec((None, 1, HGRN_DV), lambda b, h: (layer, 0, 0)),
            pl.BlockSpec(tri.shape, lambda b, h: (0, 0, 0)),
            pl.BlockSpec(mask.shape, lambda b, h: (0, 0, 0, 0)),
        ],
        out_specs=head,
        out_shape=jax.ShapeDtypeStruct((bsz, seq, V_WIDTH), F32),
        scratch_shapes=[
            pltpu.VMEM((seq, HGRN_DV), F32),
            pltpu.VMEM((seq, HGRN_DK), F32),
            pltpu.VMEM((n_chunks, HGRN_DV, HGRN_DK), F32),
            pltpu.VMEM((n_chunks, HGRN_DV, HGRN_DK), BF16),
            pltpu.VMEM((n_chunks, V7X_SUBLANES, HGRN_DK), F32),
        ],
        compiler_params=_cparams(("parallel", "parallel")),
        name="hgrn",
    )(as3(q), as3(kf), as3(kb), as3(lf), as3(lb), as3(v), as3(gate), out_norm, tri, mask)
    return out.reshape(bsz * seq, V_WIDTH)


def _outproj_kernel(yf_ref, yr_ref, x_ref, w_ref, nw_ref, wr_ref, x2_ref, h2_ref, aff_ref):
    x2 = (x_ref[...]
          + jnp.dot(yf_ref[...].astype(BF16), w_ref[:FNET_WIDTH, :], preferred_element_type=F32)
          + jnp.dot(yr_ref[...].astype(BF16), w_ref[FNET_WIDTH:, :], preferred_element_type=F32))
    x2_ref[...] = x2
    ms = jnp.mean(x2 * x2, axis=-1, keepdims=True)
    h2 = x2 * lax.rsqrt(ms + RMS_EPS) * nw_ref[...]
    h2_ref[...] = h2
    logits = lax.dot_general(wr_ref[...], h2, (((1,), (1,)), ((), ())), preferred_element_type=F32,
                             precision=lax.Precision.HIGHEST)
    e = jnp.exp(logits - jnp.max(logits, axis=0, keepdims=True))
    aff = e / jnp.sum(e, axis=0, keepdims=True)
    for j in range(aff_ref.shape[0]):
        aff_ref[j] = aff[:, j * ROUTE_CHUNK:(j + 1) * ROUTE_CHUNK]


def _outproj(yf, yr, x, w_out_bf16, norm_w, w_router_t, layer):
    n_tok = x.shape[0]
    tm = min(ROW_TILE, n_tok)
    row = lambda i: (i, 0)
    lay3 = lambda i: (layer, 0, 0)
    return pl.pallas_call(
        _outproj_kernel,
        grid=(n_tok // tm,),
        in_specs=[
            pl.BlockSpec((tm, FNET_WIDTH), row),
            pl.BlockSpec((tm, V_WIDTH), row),
            pl.BlockSpec((tm, D_MODEL), row),
            pl.BlockSpec((None, D_MODEL, D_MODEL), lay3),
            pl.BlockSpec((None, 1, D_MODEL), lay3),
            pl.BlockSpec((None, N_EXPERTS, D_MODEL), lay3),
        ],
        out_specs=[
            pl.BlockSpec((tm, D_MODEL), row),
            pl.BlockSpec((tm, D_MODEL), row),
            pl.BlockSpec((tm // ROUTE_CHUNK, N_EXPERTS, ROUTE_CHUNK), lambda i: (i, 0, 0)),
        ],
        out_shape=[
            jax.ShapeDtypeStruct((n_tok, D_MODEL), F32),
            jax.ShapeDtypeStruct((n_tok, D_MODEL), F32),
            jax.ShapeDtypeStruct((n_tok // ROUTE_CHUNK, N_EXPERTS, ROUTE_CHUNK), F32),
        ],
        compiler_params=_cparams(("parallel",)),
        name="outproj",
    )(yf, yr, x, w_out_bf16, norm_w, w_router_t)


ONE_F32_BITS = 0x3F800000
BISECT_STEPS = 31


def _select_kernel(aff_ref, tri_ref, slot_ref, off_ref, *, cap):
    n_chunks = aff_ref.shape[0]
    bits = pltpu.bitcast(aff_ref[...], jnp.int32)

    def count(pred):
        return jnp.sum(jnp.sum(pred.astype(F32), axis=0), axis=1, keepdims=True)

    def bisect(_, lo_hi):
        lo, hi = lo_hi
        mid = lo + ((hi - lo + 1) >> 1)
        ok = count(bits >= mid[None]) >= cap
        return jnp.where(ok, mid, lo), jnp.where(ok, hi, mid - 1)

    cut, _ = lax.fori_loop(0, BISECT_STEPS, bisect,
                           (jnp.zeros((N_EXPERTS, 1), jnp.int32),
                            jnp.full((N_EXPERTS, 1), ONE_F32_BITS, jnp.int32)))
    ties_wanted = cap - count(bits > cut[None])

    def scan(c, carry):
        n_sel, n_tie = carry
        b = pltpu.bitcast(aff_ref[c], jnp.int32)
        above = b > cut
        tie = (b == cut).astype(F32)
        tie_rank = n_tie + jnp.dot(tie.astype(BF16), tri_ref[...], preferred_element_type=F32) - tie
        sel = jnp.where(above, 1.0, jnp.where(tie_rank < ties_wanted, tie, 0.0))
        slot = n_sel + jnp.dot(sel.astype(BF16), tri_ref[...], preferred_element_type=F32) - sel
        slot_ref[c] = jnp.where(sel > 0.0, slot, -1.0).astype(jnp.int32)
        off_ref[c] = jnp.broadcast_to(n_sel, (N_EXPERTS, ROUTE_CHUNK)).astype(jnp.int32)
        return (n_sel + jnp.sum(sel, axis=1, keepdims=True), n_tie + jnp.sum(tie, axis=1, keepdims=True))

    zero = jnp.zeros((N_EXPERTS, 1), F32)
    lax.fori_loop(0, n_chunks, scan, (zero, zero))


@functools.lru_cache(maxsize=None)
def _prefix_table():
    t = np.arange(ROUTE_CHUNK)
    return jnp.asarray((t[:, None] <= t[None, :]).astype(np.float32), dtype=BF16)


def _select(aff, cap):
    n_chunks = aff.shape[0]
    sds = jax.ShapeDtypeStruct((n_chunks, N_EXPERTS, ROUTE_CHUNK), jnp.int32)
    return pl.pallas_call(
        functools.partial(_select_kernel, cap=cap),
        out_shape=[sds, sds],
        compiler_params=pltpu.CompilerParams(vmem_limit_bytes=V7X_VMEM_LIMIT_BYTES),
        name="route_select",
    )(aff, _prefix_table())


PLACE_ROWS = 8


def _place_kernel(off_sm, slot_ref, aff_ref, idx_ref, gate_ref, acc_ref):
    n_chunks = slot_ref.shape[0]
    n_tiles = idx_ref.shape[1]
    acc_ref[...] = jnp.zeros_like(acc_ref)
    lane = lax.broadcasted_iota(jnp.int32, (1, ROUTE_CHUNK), 1)
    j_iota = lax.broadcasted_iota(jnp.int32, (ROUTE_CHUNK, ROUTE_CHUNK), 0)
    lane8 = lax.broadcasted_iota(jnp.int32, (PLACE_ROWS, ROUTE_CHUNK), 1)
    lo_row = lane.astype(F32)

    def chunk(c, carry):
        slots = slot_ref[c]
        aff = aff_ref[c]
        hi_row = jnp.full((1, ROUTE_CHUNK), c, jnp.int32).astype(F32)
        for e in range(N_EXPERTS):
            off = off_sm[c * N_EXPERTS + e]
            tile = off >> 7
            shift = off & (ROUTE_CHUNK - 1)
            local = slots[e:e + 1] - off
            onehot = jnp.where(local == j_iota, 1.0, 0.0).astype(BF16)
            a = aff[e:e + 1]
            g1 = a.astype(BF16)
            g2 = (a - g1.astype(F32)).astype(BF16)
            g3 = (a - g1.astype(F32) - g2.astype(F32)).astype(BF16)
            lhs = jnp.concatenate([hi_row.astype(BF16), lo_row.astype(BF16), g1, g2, g3,
                                   jnp.zeros((PLACE_ROWS - 5, ROUTE_CHUNK), BF16)], axis=0)
            packed = lax.dot_general(lhs, onehot, (((1,), (1,)), ((), ())), preferred_element_type=F32)
            rolled = pltpu.roll(packed, shift, 1)
            acc_ref[e, tile] = acc_ref[e, tile] + jnp.where(lane8 >= shift, rolled, 0.0)
            acc_ref[e, tile + 1] = acc_ref[e, tile + 1] + jnp.where(lane8 < shift, rolled, 0.0)
        return carry

    lax.fori_loop(0, n_chunks, chunk, 0)
    for e in range(N_EXPERTS):
        hi = acc_ref[e, :, 0, :]
        lo = acc_ref[e, :, 1, :]
        idx_ref[e] = (hi * float(ROUTE_CHUNK) + lo).astype(jnp.int32)[:n_tiles]
        gate_ref[e] = (acc_ref[e, :, 2, :] + acc_ref[e, :, 3, :] + acc_ref[e, :, 4, :])[:n_tiles]


def _place(offsets_flat, slots, aff, cap):
    n_tiles = cap // ROUTE_CHUNK
    return pl.pallas_call(
        _place_kernel,
        grid_spec=pltpu.PrefetchScalarGridSpec(
            num_scalar_prefetch=1,
            grid=(1,),
            in_specs=[pl.BlockSpec(slots.shape, lambda i, off: (0, 0, 0)),
                      pl.BlockSpec(aff.shape, lambda i, off: (0, 0, 0))],
            out_specs=[pl.BlockSpec((N_EXPERTS, n_tiles, ROUTE_CHUNK), lambda i, off: (0, 0, 0))] * 2,
            scratch_shapes=[pltpu.VMEM((N_EXPERTS, n_tiles + 2, PLACE_ROWS, ROUTE_CHUNK), F32)],
        ),
        out_shape=[jax.ShapeDtypeStruct((N_EXPERTS, n_tiles, ROUTE_CHUNK), jnp.int32),
                   jax.ShapeDtypeStruct((N_EXPERTS, n_tiles, ROUTE_CHUNK), F32)],
        compiler_params=_cparams(("arbitrary",)),
        name="route_place",
    )(offsets_flat, slots, aff)


FFN_CHUNKS = 8


def _ffn_kernel(idx_sm, h_hbm, gate_ref, wg_ref, wu_ref, wd_ref, y_ref, xbuf, sem):
    n_k = pl.num_programs(1)
    step = pl.program_id(0) * n_k + pl.program_id(1)
    n_steps = pl.num_programs(0) * n_k
    slot = step % 2

    def row_copy(tile, buf, j):
        row = idx_sm[tile * SLOT_TILE + j]
        return pltpu.make_async_copy(h_hbm.at[pl.ds(row, 1)], xbuf.at[buf, pl.ds(j, 1)], sem.at[buf])

    def gather(tile, buf):
        def body(j, carry):
            row_copy(tile, buf, j).start()
            return carry
        lax.fori_loop(0, SLOT_TILE, body, 0)

    def wait_tile(buf):
        pltpu.make_async_copy(h_hbm.at[pl.ds(0, SLOT_TILE)], xbuf.at[buf], sem.at[buf]).wait()

    @pl.when(step == 0)
    def _():
        gather(0, 0)

    wait_tile(slot)
    x = xbuf[slot].astype(BF16)

    next_tile = jnp.where(step + 1 < n_steps, step + 1, 0)
    rows_per_chunk = SLOT_TILE // FFN_CHUNKS
    width = D_FF_EXPERT // FFN_CHUNKS
    acc = None
    for f in range(FFN_CHUNKS):
        for j in range(f * rows_per_chunk, (f + 1) * rows_per_chunk):
            row_copy(next_tile, 1 - slot, j).start()
        cols = slice(f * width, (f + 1) * width)
        hg = jnp.dot(x, wg_ref[:, cols], preferred_element_type=F32)
        hu = jnp.dot(x, wu_ref[:, cols], preferred_element_type=F32)
        he = (hg * _sigmoid(hg) * hu).astype(BF16)
        part = jnp.dot(he, wd_ref[cols, :], preferred_element_type=F32)
        acc = part if acc is None else acc + part
    y_ref[...] = acc * gate_ref[...]

    @pl.when(step + 1 == n_steps)
    def _():
        wait_tile(1 - slot)


def _ffn(idx_flat, h, gate_col, wg_bf16, wu_bf16, wd_bf16, layer, cap):
    n_k = cap // SLOT_TILE
    wspec = lambda shape: pl.BlockSpec((None, None) + shape, lambda e, k, idx: (layer, e, 0, 0))
    tile = lambda e, k, idx: (e * n_k + k, 0)
    return pl.pallas_call(
        _ffn_kernel,
        grid_spec=pltpu.PrefetchScalarGridSpec(
            num_scalar_prefetch=1,
            grid=(N_EXPERTS, n_k),
            in_specs=[pl.BlockSpec(memory_space=pl.ANY),
                      pl.BlockSpec((SLOT_TILE, 1), tile),
                      wspec((D_MODEL, D_FF_EXPERT)), wspec((D_MODEL, D_FF_EXPERT)),
                      wspec((D_FF_EXPERT, D_MODEL))],
            out_specs=pl.BlockSpec((SLOT_TILE, D_MODEL), tile),
            scratch_shapes=[pltpu.VMEM((2, SLOT_TILE, D_MODEL), F32), pltpu.SemaphoreType.DMA((2,))],
        ),
        out_shape=jax.ShapeDtypeStruct((N_EXPERTS * cap, D_MODEL), F32),
        compiler_params=_cparams(("arbitrary", "arbitrary")),
        name="expert_ffn",
    )(idx_flat, h, gate_col, wg_bf16, wu_bf16, wd_bf16)


PIECE = 32
PIECES_PER_EXPERT = -(-(ROUTE_CHUNK + 2 * (V7X_SUBLANES - 1)) // PIECE)
PACK_ROWS = N_EXPERTS * PIECES_PER_EXPERT * PIECE
PACK_BLOCK = 256


def _combine_kernel(off_sm, x_ref, slot_ref, y_hbm, o_ref, buf, sem, *, cap):
    i = pl.program_id(0)
    n_tiles = pl.num_programs(0)
    cur = i % 2

    def plan(tile):
        starts, pieces, bases = [], [], []
        base = 0
        for e in range(N_EXPERTS):
            s0 = off_sm[tile * N_EXPERTS + e]
            s1 = off_sm[(tile + 1) * N_EXPERTS + e]
            a0 = (s0 >> 3) << 3
            rows = jnp.where(s1 > s0, (((s1 + V7X_SUBLANES - 1) >> 3) << 3) - a0, 0)
            n_p = (rows + PIECE - 1) // PIECE
            starts.append(jnp.minimum(a0, cap - n_p * PIECE))
            pieces.append(n_p)
            bases.append(base)
            base = base + n_p * PIECE
        return starts, pieces, bases, base

    def piece_copy(src_row, b, dst_row):
        return pltpu.make_async_copy(y_hbm.at[pl.ds(pl.multiple_of(src_row, V7X_SUBLANES), PIECE)],
                                     buf.at[b, pl.ds(pl.multiple_of(dst_row, V7X_SUBLANES), PIECE)], sem.at[b])

    def fetch(tile, b):
        starts, pieces, bases, _ = plan(tile)
        for e in range(N_EXPERTS):
            def body(j, carry, e=e):
                piece_copy(e * cap + starts[e] + j * PIECE, b, bases[e] + j * PIECE).start()
                return carry
            lax.fori_loop(0, pieces[e], body, 0)

    @pl.when(i == 0)
    def _():
        buf[...] = jnp.zeros_like(buf)
        fetch(0, 0)

    @pl.when(i + 1 < n_tiles)
    def _():
        fetch(i + 1, 1 - cur)

    starts, pieces, bases, total = plan(i)

    def wait_piece(j, carry):
        piece_copy(0, cur, 0).wait()
        return carry
    lax.fori_loop(0, total // PIECE, wait_piece, 0)

    slots = slot_ref[...].astype(F32)
    slots_t = jnp.concatenate([slots, jnp.zeros((ROUTE_CHUNK - N_EXPERTS, ROUTE_CHUNK), F32)], axis=0).T
    packed_row = []
    for e in range(N_EXPERTS):
        col = slots_t[:, e:e + 1]
        shift = (bases[e] - starts[e]).astype(F32)
        packed_row.append(jnp.where(col >= 0.0, col + shift, -1.0))

    o_ref[...] = x_ref[...]
    lane = lax.broadcasted_iota(jnp.int32, (1, PACK_BLOCK), 1).astype(F32)

    def block(kb, carry):
        rows = pl.ds(pl.multiple_of(kb * PACK_BLOCK, PACK_BLOCK), PACK_BLOCK)
        row_id = lane + (kb * PACK_BLOCK).astype(F32)
        onehot = jnp.zeros((ROUTE_CHUNK, PACK_BLOCK), F32)
        for e in range(N_EXPERTS):
            onehot = onehot + jnp.where(packed_row[e] == row_id, 1.0, 0.0)
        onehot = onehot.astype(BF16)
        y = buf[cur, rows, :]
        y_hi = y.astype(BF16)
        y_lo = (y - y_hi.astype(F32)).astype(BF16)
        o_ref[...] = (o_ref[...] + jnp.dot(onehot, y_hi, preferred_element_type=F32)
                      + jnp.dot(onehot, y_lo, preferred_element_type=F32))
        return carry
    lax.fori_loop(0, (total + PACK_BLOCK - 1) // PACK_BLOCK, block, 0)


def _combine(offsets_flat, x, slots, ye, cap):
    n_tok = x.shape[0]
    tok = pl.BlockSpec((ROUTE_CHUNK, D_MODEL), lambda i, off: (i, 0))
    return pl.pallas_call(
        functools.partial(_combine_kernel, cap=cap),
        grid_spec=pltpu.PrefetchScalarGridSpec(
            num_scalar_prefetch=1,
            grid=(n_tok // ROUTE_CHUNK,),
            in_specs=[tok,
                      pl.BlockSpec((None, N_EXPERTS, ROUTE_CHUNK), lambda i, off: (i, 0, 0)),
                      pl.BlockSpec(memory_space=pl.ANY)],
            out_specs=tok,
            scratch_shapes=[pltpu.VMEM((2, PACK_ROWS, D_MODEL), F32), pltpu.SemaphoreType.DMA((2,))],
        ),
        out_shape=jax.ShapeDtypeStruct((n_tok, D_MODEL), F32),
        compiler_params=_cparams(("arbitrary",)),
        name="combine",
    )(offsets_flat, x, slots, ye)


def _final_norm_kernel(x_ref, nw_ref, o_ref):
    x = x_ref[...]
    ms = jnp.mean(x * x, axis=-1, keepdims=True)
    o_ref[...] = x * lax.rsqrt(ms + RMS_EPS) * nw_ref[...]


def _final_norm(x, norm_w):
    n_tok = x.shape[0]
    tm = min(ROW_TILE, n_tok)
    return pl.pallas_call(
        _final_norm_kernel,
        grid=(n_tok // tm,),
        in_specs=[pl.BlockSpec((tm, D_MODEL), lambda i: (i, 0)), pl.BlockSpec((1, D_MODEL), lambda i: (0, 0))],
        out_specs=pl.BlockSpec((tm, D_MODEL), lambda i: (i, 0)),
        out_shape=jax.ShapeDtypeStruct((n_tok, D_MODEL), F32),
        compiler_params=_cparams(("parallel",)),
        name="final_norm",
    )(x, norm_w)


def _moe(x2, h2, aff, params, layer):
    n_tok = x2.shape[0]
    cap = EC_CAPACITY_FACTOR * n_tok // N_EXPERTS
    slots, offs = _select(aff, cap)
    offsets_flat = jnp.concatenate(
        [offs[:, :, 0], jnp.full((1, N_EXPERTS), cap, jnp.int32)], axis=0).reshape(-1)
    idx, gate = _place(offsets_flat, slots, aff, cap)
    ye = _ffn(idx.reshape(-1), h2, gate.reshape(-1, 1), params["w_gate"], params["w_up"], params["w_down"],
              layer, cap)
    return _combine(offsets_flat, x2, slots, ye, cap)


def _trunk(x, params):
    bsz, seq, _ = x.shape
    x = x.reshape(bsz * seq, D_MODEL)
    for layer in range(DEPTH):
        u, q, kf, kb, lf, lb, v, g = _inproj(x, params["norm_mix"], params["w_in"], params["lbf"], params["lbb"], layer)
        yf = _fnet(u, params["w_fnet"], layer, bsz, seq)
        yr = _hgrn(q, kf, kb, lf, lb, v, g, params["hgrn_out_norm"], layer, bsz, seq)
        x2, h2, aff = _outproj(yf, yr, x, params["w_out"], params["norm_ffn"], params["w_router_t"], layer)
        x = _moe(x2, h2, aff, params, layer)
    return _final_norm(x, params["norm_final"]).reshape(bsz, seq, D_MODEL)


def kernel(x_prompt, x_sample, norm_mix, w_in, w_fnet, lb_fwd, lb_bwd, hgrn_out_norm, w_out,
           norm_ffn, w_router, w_gate, w_up, w_down, norm_final):
    params = {
        "norm_mix": norm_mix.reshape(DEPTH, 1, D_MODEL),
        "w_in": w_in.astype(BF16),
        "w_fnet": w_fnet.astype(BF16),
        "lbf": _lower_bounds(lb_fwd).reshape(DEPTH, 1, QK_WIDTH),
        "lbb": _lower_bounds(lb_bwd).reshape(DEPTH, 1, QK_WIDTH),
        "hgrn_out_norm": hgrn_out_norm.reshape(DEPTH, 1, HGRN_DV),
        "w_out": w_out.astype(BF16),
        "norm_ffn": norm_ffn.reshape(DEPTH, 1, D_MODEL),
        "w_router_t": jnp.swapaxes(w_router, 1, 2),
        "w_gate": w_gate.astype(BF16),
        "w_up": w_up.astype(BF16),
        "w_down": w_down.astype(BF16),
        "norm_final": norm_final.reshape(1, D_MODEL),
    }
    return _trunk(x_prompt, params), _trunk(x_sample, params)
```

```python
import functools

import numpy as np
import jax
import jax.numpy as jnp
from jax import lax
from jax.experimental import pallas as pl
from jax.experimental.pallas import tpu as pltpu

D_MODEL = 1024
DEPTH = 4
FNET_WIDTH = 512
FNET_GROUPS = 4
FNET_GROUP_DIM = 128
HGRN_HEADS = 4
HGRN_DK = 128
HGRN_DV = 128
QK_WIDTH = 512
V_WIDTH = 512
IN_WIDTH = FNET_WIDTH + 3 * QK_WIDTH + 2 * V_WIDTH
N_EXPERTS = 16
EC_CAPACITY_FACTOR = 2
D_FF_EXPERT = 2048
RMS_EPS = 1e-6

V7X_LANES = 128
V7X_SUBLANES = 8
V7X_VMEM_LIMIT_BYTES = 56 * 1024 * 1024

BF16 = jnp.bfloat16
F32 = jnp.float32

CHUNK = 64

ROW_TILE = 512
SLOT_TILE = 256
ROUTE_CHUNK = 128


def _cparams(sem, vmem=V7X_VMEM_LIMIT_BYTES):
    return pltpu.CompilerParams(dimension_semantics=sem, vmem_limit_bytes=vmem)


def _sigmoid(x):
    return 1.0 / (1.0 + jnp.exp(-x))


def _lower_bounds_kernel(p_ref, o_ref):
    p = p_ref[...]
    m = jnp.max(p, axis=0, keepdims=True)
    e = jnp.exp(p - m)
    sm = e / jnp.sum(e, axis=0, keepdims=True)
    acc = None
    rows = []
    for layer in range(DEPTH):
        acc = sm[layer:layer + 1] if acc is None else acc + sm[layer:layer + 1]
        rows.append(jnp.maximum(acc - sm[0:1], 0.0))
    o_ref[...] = jnp.concatenate(rows, axis=0)


def _lower_bounds(lb_param):
    return pl.pallas_call(
        _lower_bounds_kernel,
        out_shape=jax.ShapeDtypeStruct((DEPTH, QK_WIDTH), F32),
        name="lower_bounds",
    )(lb_param)


def _hgrn_gates(z, lb):
    e = jnp.exp(-jnp.abs(z))
    r = 1.0 / (1.0 + e)
    sig_neg = jnp.where(z >= 0, e * r, r)
    k = (1.0 - lb) * sig_neg
    log_sig = jnp.minimum(z, 0.0) - jnp.log(1.0 + e)
    c = jnp.log1p(-lb) + log_sig
    has_lb = lb > 0.0
    a = jnp.log(jnp.where(has_lb, lb, 1.0))
    logf = jnp.where(has_lb, jnp.maximum(a, c) + jnp.log(1.0 + jnp.exp(-jnp.abs(a - c))), c)
    return k, logf


def _inproj_kernel(x_ref, nw_ref, w_ref, lbf_ref, lbb_ref,
                   u_ref, q_ref, kf_ref, kb_ref, lf_ref, lbo_ref, v_ref, g_ref):
    x = x_ref[...]
    ms = jnp.mean(x * x, axis=-1, keepdims=True)
    h = (x * lax.rsqrt(ms + RMS_EPS) * nw_ref[...]).astype(BF16)

    def proj(col0, width):
        return jnp.dot(h, w_ref[:, col0:col0 + width], preferred_element_type=F32)

    u_ref[...] = proj(0, FNET_WIDTH)
    q = proj(FNET_WIDTH, QK_WIDTH)
    q_ref[...] = q * _sigmoid(q)
    kf, lf = _hgrn_gates(proj(FNET_WIDTH + QK_WIDTH, QK_WIDTH), lbf_ref[...])
    kf_ref[...] = kf
    lf_ref[...] = lf
    kb, lb = _hgrn_gates(proj(FNET_WIDTH + 2 * QK_WIDTH, QK_WIDTH), lbb_ref[...])
    kb_ref[...] = kb
    lbo_ref[...] = lb
    v_ref[...] = proj(FNET_WIDTH + 3 * QK_WIDTH, V_WIDTH)
    g = proj(FNET_WIDTH + 3 * QK_WIDTH + V_WIDTH, V_WIDTH)
    g_ref[...] = g * _sigmoid(g)


def _inproj(x, norm_w, w_in_bf16, lbf, lbb, layer):
    n_tok = x.shape[0]
    tm = min(ROW_TILE, n_tok)
    row = lambda i: (i, 0)
    lay3 = lambda i: (layer, 0, 0)
    out_sds = jax.ShapeDtypeStruct((n_tok, QK_WIDTH), F32)
    return pl.pallas_call(
        _inproj_kernel,
        grid=(n_tok // tm,),
        in_specs=[
            pl.BlockSpec((tm, D_MODEL), row),
            pl.BlockSpec((None, 1, D_MODEL), lay3),
            pl.BlockSpec((None, D_MODEL, IN_WIDTH), lay3),
            pl.BlockSpec((None, 1, QK_WIDTH), lay3),
            pl.BlockSpec((None, 1, QK_WIDTH), lay3),
        ],
        out_specs=[pl.BlockSpec((tm, QK_WIDTH), row)] * 8,
        out_shape=[out_sds] * 8,
        compiler_params=_cparams(("parallel",)),
        name="inproj",
    )(x, norm_w, w_in_bf16, lbf, lbb)


FNET_RADIX = 4


@functools.lru_cache(maxsize=None)
def _fnet_tables(seq):
    m = seq // FNET_RADIX
    k = np.arange(m, dtype=np.float64)
    ang = 2.0 * np.pi * np.outer(k, k) / m
    cs = np.concatenate([np.cos(ang), np.sin(ang)], axis=0)
    tw = []
    for r in range(1, FNET_RADIX):
        a = 2.0 * np.pi * r * k / seq
        tw.append(np.broadcast_to(np.cos(a)[:, None], (m, V7X_LANES)))
        tw.append(np.broadcast_to(np.sin(a)[:, None], (m, V7X_LANES)))
    tw = np.stack(tw, axis=0)
    c = np.arange(FNET_GROUP_DIM, dtype=np.float64)
    angc = 2.0 * np.pi * np.outer(c, c) / FNET_GROUP_DIM
    scale = 1.0 / np.sqrt(float(seq) * FNET_GROUP_DIM)
    ch = np.concatenate([np.cos(angc), np.sin(angc)], axis=0) * scale
    return (jnp.asarray(cs, dtype=BF16), jnp.asarray(tw, dtype=F32), jnp.asarray(ch, dtype=BF16))


def _fnet_kernel(x0_ref, x1_ref, x2_ref, x3_ref, cs_ref, tw_ref, ch_ref, w_ref, o_ref):
    m = x0_ref.shape[0]
    x = jnp.concatenate([r[...].astype(BF16) for r in (x0_ref, x1_ref, x2_ref, x3_ref)], axis=1)
    pq = jnp.dot(cs_ref[...], x, preferred_element_type=F32)
    h_re, h_im = [], []
    for r in range(FNET_RADIX):
        p = pq[:m, r * V7X_LANES:(r + 1) * V7X_LANES]
        q = pq[m:, r * V7X_LANES:(r + 1) * V7X_LANES]
        if r == 0:
            h_re.append(p)
            h_im.append(-q)
        else:
            tc = tw_ref[2 * (r - 1)]
            ts = tw_ref[2 * (r - 1) + 1]
            h_re.append(tc * p - ts * q)
            h_im.append(-(tc * q + ts * p))
    x_re = [h_re[0] + h_re[1] + h_re[2] + h_re[3],
            h_re[0] + h_im[1] - h_re[2] - h_im[3],
            h_re[0] - h_re[1] + h_re[2] - h_re[3],
            h_re[0] - h_im[1] - h_re[2] + h_im[3]]
    x_im = [h_im[0] + h_im[1] + h_im[2] + h_im[3],
            h_im[0] - h_re[1] - h_im[2] + h_re[3],
            h_im[0] - h_im[1] + h_im[2] - h_im[3],
            h_im[0] + h_re[1] - h_im[2] - h_re[3]]
    xc = jnp.concatenate([jnp.concatenate([a, b], axis=1) for a, b in zip(x_re, x_im)], axis=0)
    mixed = jnp.dot(xc.astype(BF16), ch_ref[...], preferred_element_type=F32)
    o_ref[...] = jnp.dot(mixed.astype(BF16), w_ref[...], preferred_element_type=F32)


def _fnet(u, w_fnet_bf16, layer, bsz, seq):
    m = seq // FNET_RADIX
    cs, tw, ch = _fnet_tables(seq)
    u4 = u.reshape(bsz, m, FNET_RADIX * FNET_WIDTH)
    x_specs = [pl.BlockSpec((None, m, FNET_GROUP_DIM),
                            functools.partial(lambda b, g, r: (b, 0, r * FNET_GROUPS + g), r=r))
               for r in range(FNET_RADIX)]
    out = pl.pallas_call(
        _fnet_kernel,
        grid=(bsz, FNET_GROUPS),
        in_specs=x_specs + [
            pl.BlockSpec((2 * m, m), lambda b, g: (0, 0)),
            pl.BlockSpec((2 * (FNET_RADIX - 1), m, V7X_LANES), lambda b, g: (0, 0, 0)),
            pl.BlockSpec((2 * FNET_GROUP_DIM, FNET_GROUP_DIM), lambda b, g: (0, 0)),
            pl.BlockSpec((None, None, FNET_GROUP_DIM, FNET_GROUP_DIM), lambda b, g: (layer, g, 0, 0)),
        ],
        out_specs=pl.BlockSpec((None, seq, FNET_GROUP_DIM), lambda b, g: (b, 0, g)),
        out_shape=jax.ShapeDtypeStruct((bsz, seq, FNET_WIDTH), F32),
        compiler_params=_cparams(("parallel", "parallel")),
        name="fnet",
    )(u4, u4, u4, u4, cs, tw, ch, w_fnet_bf16)
    return out.reshape(bsz * seq, FNET_WIDTH)


PAIR_LEVELS = ((CHUNK, 16), (16, 4))
PAIR_PARTS = 3
BAND = 4
CHUNK_UNROLL = 8
LOG2_E = 1.4426950408889634


@functools.lru_cache(maxsize=None)
def _hgrn_tables():
    t = np.arange(CHUNK)
    tri = np.stack([(t[None, :] <= t[:, None]), (t[None, :] >= t[:, None])]).astype(np.float32)
    mask = np.zeros((2, len(PAIR_LEVELS), CHUNK, PAIR_PARTS * CHUNK), np.float32)
    for lvl, (outer, inner) in enumerate(PAIR_LEVELS):
        same = (t[:, None] // outer) == (t[None, :] // outer)
        t_in = (t[:, None] % outer) // inner
        s_in = t[None, :] % outer
        for j in range(PAIR_PARTS):
            cols = slice(j * CHUNK, (j + 1) * CHUNK)
            mask[0, lvl, :, cols] = same & (t_in == j + 1) & (s_in < inner * (j + 1))
            mask[1, lvl, :, cols] = same & (t_in == j) & (s_in >= inner * (j + 1))
    return jnp.asarray(tri), jnp.asarray(mask)


def _rows_block(p, rows, block):
    def rep(r, n):
        if 0 <= r < CHUNK:
            return jnp.broadcast_to(p[r:r + 1], (n, HGRN_DK))
        return jnp.zeros((n, HGRN_DK), F32)

    if block >= V7X_SUBLANES:
        return jnp.concatenate([rep(r, block) for r in rows], axis=0)
    upper = lax.broadcasted_iota(jnp.int32, (V7X_SUBLANES, HGRN_DK), 0) < block
    return jnp.concatenate([jnp.where(upper, rep(rows[2 * i], V7X_SUBLANES), rep(rows[2 * i + 1], V7X_SUBLANES))
                            for i in range(len(rows) // 2)], axis=0)


def _level_operands(q, k, p, reverse, outer, inner):
    q_rows = [inner * (ib + 1) if reverse else inner * ib - 1 for ib in range(CHUNK // inner)]
    q_l = (q * jnp.exp2(p - _rows_block(p, q_rows, inner))).astype(BF16)
    parts = []
    for j in range(PAIR_PARTS):
        k_rows = [outer * ob + inner * (j + 1) - (0 if reverse else 1) for ob in range(CHUNK // outer)]
        parts.append((k * jnp.exp2(jnp.minimum(_rows_block(p, k_rows, outer) - p, 0.0))).astype(BF16))
    return q_l, jnp.concatenate(parts, axis=0)


def _chunk_rows(c):
    return pl.ds(pl.multiple_of(c * CHUNK, CHUNK), CHUNK)


def _hgrn_prepare(cs, reverse, k_ref, g_ref, v_ref, tri_ref, p_ref, ut_ref, de_ref):
    rows = [_chunk_rows(c) for c in cs]
    g_all = jnp.concatenate([g_ref[r, :] for r in rows], axis=1)
    p_all = LOG2_E * jnp.dot(tri_ref[1 if reverse else 0], g_all, preferred_element_type=F32,
                             precision=lax.Precision.HIGHEST)
    ps = [p_all[:, i * HGRN_DK:(i + 1) * HGRN_DK] for i in range(len(cs))]
    edges = [p[0:1] if reverse else p[CHUNK - 1:CHUNK] for p in ps]
    kdecs = [(k_ref[r, :] * jnp.exp2(e - p)).astype(BF16) for r, e, p in zip(rows, edges, ps)]
    uts = [lax.dot_general(v_ref[r, :].astype(BF16), kd, (((0,), (0,)), ((), ())), preferred_element_type=F32)
           for r, kd in zip(rows, kdecs)]
    for c, r, p, e, ut in zip(cs, rows, ps, edges, uts):
        p_ref[r, :] = p
        ut_ref[c] = ut
        de_ref[c] = jnp.broadcast_to(jnp.exp2(e), (V7X_SUBLANES, HGRN_DK))


def _hgrn_scan(i, state, reverse, n_chunks, st_ref, ut_ref, de_ref):
    c = n_chunks - 1 - i if reverse else i
    st_ref[c] = state.astype(BF16)
    return state * de_ref[c][0:1] + ut_ref[c]


def _hgrn_output(cs, reverse, q_ref, k_ref, v_ref, p_ref, st_ref, mask_ref, o_ref):
    d = 1 if reverse else 0
    n = len(cs)
    nt = (((1,), (1,)), ((), ()))
    rows = [_chunk_rows(c) for c in cs]
    qs = [q_ref[r, :] for r in rows]
    ks = [k_ref[r, :] for r in rows]
    vs = [v_ref[r, :] for r in rows]
    ps = [p_ref[r, :] for r in rows]
    v_parts = [jnp.concatenate([v.astype(BF16)] * PAIR_PARTS, axis=0) for v in vs]
    outs = [lax.dot_general((qs[i] * jnp.exp2(ps[i])).astype(BF16), st_ref[cs[i]], nt, preferred_element_type=F32)
            for i in range(n)]
    for lvl, (outer, inner) in enumerate(PAIR_LEVELS):
        operands = [_level_operands(qs[i], ks[i], ps[i], reverse, outer, inner) for i in range(n)]
        scores = [lax.dot_general(q_l, k_l, nt, preferred_element_type=F32) for q_l, k_l in operands]
        weights = [(s * mask_ref[d, lvl]).astype(BF16) for s in scores]
        outs = [outs[i] + jnp.dot(weights[i], v_parts[i], preferred_element_type=F32) for i in range(n)]

    t_in = lax.broadcasted_iota(jnp.int32, (CHUNK, 1), 0) % BAND
    for i in range(n):
        q, k, v, p = qs[i], ks[i], vs[i], ps[i]
        out = outs[i] + jnp.sum(q * k, axis=-1, keepdims=True) * v
        for delta in range(1, BAND):
            shift = CHUNK - delta if reverse else delta
            k_s = pltpu.roll(k, shift, 0)
            p_s = pltpu.roll(p, shift, 0)
            v_s = pltpu.roll(v, shift, 0)
            valid = (t_in <= BAND - 1 - delta) if reverse else (t_in >= delta)
            a = jnp.sum(q * k_s * jnp.exp2(jnp.minimum(p - p_s, 0.0)), axis=-1, keepdims=True)
            out = out + jnp.where(valid, a, 0.0) * v_s
        if reverse:
            o_ref[rows[i], :] = o_ref[rows[i], :] + out
        else:
            o_ref[rows[i], :] = out


def _hgrn_kernel(q_ref, kf_ref, kb_ref, lf_ref, lb_ref, v_ref, gate_ref, nw_ref, tri_ref, mask_ref,
                 y_ref, o_ref, p_ref, ut_ref, st_ref, de_ref):
    seq = q_ref.shape[0]
    n_chunks = seq // CHUNK

    def per_chunk(fn):
        def body(i, carry):
            fn([i * CHUNK_UNROLL + u for u in range(CHUNK_UNROLL)])
            return carry
        lax.fori_loop(0, n_chunks // CHUNK_UNROLL, body, 0)

    for reverse, k_ref, g_ref in ((False, kf_ref, lf_ref), (True, kb_ref, lb_ref)):
        per_chunk(functools.partial(_hgrn_prepare, reverse=reverse, k_ref=k_ref, g_ref=g_ref, v_ref=v_ref,
                                    tri_ref=tri_ref, p_ref=p_ref, ut_ref=ut_ref, de_ref=de_ref))
        lax.fori_loop(0, n_chunks,
                      functools.partial(_hgrn_scan, reverse=reverse, n_chunks=n_chunks, st_ref=st_ref,
                                        ut_ref=ut_ref, de_ref=de_ref),
                      jnp.zeros((HGRN_DV, HGRN_DK), F32))
        per_chunk(functools.partial(_hgrn_output, reverse=reverse, q_ref=q_ref, k_ref=k_ref, v_ref=v_ref,
                                    p_ref=p_ref, st_ref=st_ref, mask_ref=mask_ref, o_ref=o_ref))
    o = o_ref[...]
    ms = jnp.mean(o * o, axis=-1, keepdims=True)
    y_ref[...] = o * lax.rsqrt(ms + RMS_EPS) * nw_ref[...] * gate_ref[...]


def _hgrn(q, kf, kb, lf, lb, v, gate, out_norm, layer, bsz, seq):
    tri, mask = _hgrn_tables()
    n_chunks = seq // CHUNK
    as3 = lambda a: a.reshape(bsz, seq, QK_WIDTH)
    head = pl.BlockSpec((None, seq, HGRN_DK), lambda b, h: (b, 0, h))
    out = pl.pallas_call(
        _hgrn_kernel,
        grid=(bsz, HGRN_HEADS),
        in_specs=[head] * 7 + [
            pl.BlockSpec((None, 1, HGRN_DV), lambda b, h: (layer, 0, 0)),
            pl.BlockSpec(tri.shape, lambda b, h: (0, 0, 0)),
            pl.BlockSpec(mask.shape, lambda b, h: (0, 0, 0, 0)),
        ],
        out_specs=head,
        out_shape=jax.ShapeDtypeStruct((bsz, seq, V_WIDTH), F32),
        scratch_shapes=[
            pltpu.VMEM((seq, HGRN_DV), F32),
            pltpu.VMEM((seq, HGRN_DK), F32),
            pltpu.VMEM((n_chunks, HGRN_DV, HGRN_DK), F32),
            pltpu.VMEM((n_chunks, HGRN_DV, HGRN_DK), BF16),
            pltpu.VMEM((n_chunks, V7X_SUBLANES, HGRN_DK), F32),
        ],
        compiler_params=_cparams(("parallel", "parallel")),
        name="hgrn",
    )(as3(q), as3(kf), as3(kb), as3(lf), as3(lb), as3(v), as3(gate), out_norm, tri, mask)
    return out.reshape(bsz * seq, V_WIDTH)


def _outproj_kernel(yf_ref, yr_ref, x_ref, w_ref, nw_ref, wr_ref, x2_ref, h2_ref, aff_ref):
    x2 = (x_ref[...]
          + jnp.dot(yf_ref[...].astype(BF16), w_ref[:FNET_WIDTH, :], preferred_element_type=F32)
          + jnp.dot(yr_ref[...].astype(BF16), w_ref[FNET_WIDTH:, :], preferred_element_type=F32))
    x2_ref[...] = x2
    ms = jnp.mean(x2 * x2, axis=-1, keepdims=True)
    h2 = x2 * lax.rsqrt(ms + RMS_EPS) * nw_ref[...]
    h2_ref[...] = h2
    logits = lax.dot_general(wr_ref[...], h2, (((1,), (1,)), ((), ())), preferred_element_type=F32,
                             precision=lax.Precision.HIGHEST)
    e = jnp.exp(logits - jnp.max(logits, axis=0, keepdims=True))
    aff = e / jnp.sum(e, axis=0, keepdims=True)
    for j in range(aff_ref.shape[0]):
        aff_ref[j] = aff[:, j * ROUTE_CHUNK:(j + 1) * ROUTE_CHUNK]


def _outproj(yf, yr, x, w_out_bf16, norm_w, w_router_t, layer):
    n_tok = x.shape[0]
    tm = min(ROW_TILE, n_tok)
    row = lambda i: (i, 0)
    lay3 = lambda i: (layer, 0, 0)
    return pl.pallas_call(
        _outproj_kernel,
        grid=(n_tok // tm,),
        in_specs=[
            pl.BlockSpec((tm, FNET_WIDTH), row),
            pl.BlockSpec((tm, V_WIDTH), row),
            pl.BlockSpec((tm, D_MODEL), row),
            pl.BlockSpec((None, D_MODEL, D_MODEL), lay3),
            pl.BlockSpec((None, 1, D_MODEL), lay3),
            pl.BlockSpec((None, N_EXPERTS, D_MODEL), lay3),
        ],
        out_specs=[
            pl.BlockSpec((tm, D_MODEL), row),
            pl.BlockSpec((tm, D_MODEL), row),
            pl.BlockSpec((tm // ROUTE_CHUNK, N_EXPERTS, ROUTE_CHUNK), lambda i: (i, 0, 0)),
        ],
        out_shape=[
            jax.ShapeDtypeStruct((n_tok, D_MODEL), F32),
            jax.ShapeDtypeStruct((n_tok, D_MODEL), F32),
            jax.ShapeDtypeStruct((n_tok // ROUTE_CHUNK, N_EXPERTS, ROUTE_CHUNK), F32),
        ],
        compiler_params=_cparams(("parallel",)),
        name="outproj",
    )(yf, yr, x, w_out_bf16, norm_w, w_router_t)


ONE_F32_BITS = 0x3F800000
BISECT_STEPS = 31


def _select_kernel(aff_ref, tri_ref, slot_ref, off_ref, *, cap):
    n_chunks = aff_ref.shape[0]
    bits = pltpu.bitcast(aff_ref[...], jnp.int32)

    def count(pred):
        return jnp.sum(jnp.sum(pred.astype(F32), axis=0), axis=1, keepdims=True)

    def bisect(_, lo_hi):
        lo, hi = lo_hi
        mid = lo + ((hi - lo + 1) >> 1)
        ok = count(bits >= mid[None]) >= cap
        return jnp.where(ok, mid, lo), jnp.where(ok, hi, mid - 1)

    cut, _ = lax.fori_loop(0, BISECT_STEPS, bisect,
                           (jnp.zeros((N_EXPERTS, 1), jnp.int32),
                            jnp.full((N_EXPERTS, 1), ONE_F32_BITS, jnp.int32)))
    ties_wanted = cap - count(bits > cut[None])

    def scan(c, carry):
        n_sel, n_tie = carry
        b = pltpu.bitcast(aff_ref[c], jnp.int32)
        above = b > cut
        tie = (b == cut).astype(F32)
        tie_rank = n_tie + jnp.dot(tie.astype(BF16), tri_ref[...], preferred_element_type=F32) - tie
        sel = jnp.where(above, 1.0, jnp.where(tie_rank < ties_wanted, tie, 0.0))
        slot = n_sel + jnp.dot(sel.astype(BF16), tri_ref[...], preferred_element_type=F32) - sel
        slot_ref[c] = jnp.where(sel > 0.0, slot, -1.0).astype(jnp.int32)
        off_ref[c] = jnp.broadcast_to(n_sel, (N_EXPERTS, ROUTE_CHUNK)).astype(jnp.int32)
        return (n_sel + jnp.sum(sel, axis=1, keepdims=True), n_tie + jnp.sum(tie, axis=1, keepdims=True))

    zero = jnp.zeros((N_EXPERTS, 1), F32)
    lax.fori_loop(0, n_chunks, scan, (zero, zero))


@functools.lru_cache(maxsize=None)
def _prefix_table():
    t = np.arange(ROUTE_CHUNK)
    return jnp.asarray((t[:, None] <= t[None, :]).astype(np.float32), dtype=BF16)


def _select(aff, cap):
    n_chunks = aff.shape[0]
    sds = jax.ShapeDtypeStruct((n_chunks, N_EXPERTS, ROUTE_CHUNK), jnp.int32)
    return pl.pallas_call(
        functools.partial(_select_kernel, cap=cap),
        out_shape=[sds, sds],
        compiler_params=pltpu.CompilerParams(vmem_limit_bytes=V7X_VMEM_LIMIT_BYTES),
        name="route_select",
    )(aff, _prefix_table())


PLACE_ROWS = 8


def _place_kernel(off_sm, slot_ref, aff_ref, idx_ref, gate_ref, acc_ref):
    n_chunks = slot_ref.shape[0]
    n_tiles = idx_ref.shape[1]
    acc_ref[...] = jnp.zeros_like(acc_ref)
    lane = lax.broadcasted_iota(jnp.int32, (1, ROUTE_CHUNK), 1)
    j_iota = lax.broadcasted_iota(jnp.int32, (ROUTE_CHUNK, ROUTE_CHUNK), 0)
    lane8 = lax.broadcasted_iota(jnp.int32, (PLACE_ROWS, ROUTE_CHUNK), 1)
    lo_row = lane.astype(F32)

    def chunk(c, carry):
        slots = slot_ref[c]
        aff = aff_ref[c]
        hi_row = jnp.full((1, ROUTE_CHUNK), c, jnp.int32).astype(F32)
        for e in range(N_EXPERTS):
            off = off_sm[c * N_EXPERTS + e]
            tile = off >> 7
            shift = off & (ROUTE_CHUNK - 1)
            local = slots[e:e + 1] - off
            onehot = jnp.where(local == j_iota, 1.0, 0.0).astype(BF16)
            a = aff[e:e + 1]
            g1 = a.astype(BF16)
            g2 = (a - g1.astype(F32)).astype(BF16)
            g3 = (a - g1.astype(F32) - g2.astype(F32)).astype(BF16)
            lhs = jnp.concatenate([hi_row.astype(BF16), lo_row.astype(BF16), g1, g2, g3,
                                   jnp.zeros((PLACE_ROWS - 5, ROUTE_CHUNK), BF16)], axis=0)
            packed = lax.dot_general(lhs, onehot, (((1,), (1,)), ((), ())), preferred_element_type=F32)
            rolled = pltpu.roll(packed, shift, 1)
            acc_ref[e, tile] = acc_ref[e, tile] + jnp.where(lane8 >= shift, rolled, 0.0)
            acc_ref[e, tile + 1] = acc_ref[e, tile + 1] + jnp.where(lane8 < shift, rolled, 0.0)
        return carry

    lax.fori_loop(0, n_chunks, chunk, 0)
    for e in range(N_EXPERTS):
        hi = acc_ref[e, :, 0, :]
        lo = acc_ref[e, :, 1, :]
        idx_ref[e] = (hi * float(ROUTE_CHUNK) + lo).astype(jnp.int32)[:n_tiles]
        gate_ref[e] = (acc_ref[e, :, 2, :] + acc_ref[e, :, 3, :] + acc_ref[e, :, 4, :])[:n_tiles]


def _place(offsets_flat, slots, aff, cap):
    n_tiles = cap // ROUTE_CHUNK
    return pl.pallas_call(
        _place_kernel,
        grid_spec=pltpu.PrefetchScalarGridSpec(
            num_scalar_prefetch=1,
            grid=(1,),
            in_specs=[pl.BlockSpec(slots.shape, lambda i, off: (0, 0, 0)),
                      pl.BlockSpec(aff.shape, lambda i, off: (0, 0, 0))],
            out_specs=[pl.BlockSpec((N_EXPERTS, n_tiles, ROUTE_CHUNK), lambda i, off: (0, 0, 0))] * 2,
            scratch_shapes=[pltpu.VMEM((N_EXPERTS, n_tiles + 2, PLACE_ROWS, ROUTE_CHUNK), F32)],
        ),
        out_shape=[jax.ShapeDtypeStruct((N_EXPERTS, n_tiles, ROUTE_CHUNK), jnp.int32),
                   jax.ShapeDtypeStruct((N_EXPERTS, n_tiles, ROUTE_CHUNK), F32)],
        compiler_params=_cparams(("arbitrary",)),
        name="route_place",
    )(offsets_flat, slots, aff)


FFN_CHUNKS = 8
FFN_BUFFERS = 3


def _ffn_kernel(idx_sm, h_hbm, gate_ref, wg_ref, wu_ref, wd_ref, y_ref, xbuf, sem):
    n_k = pl.num_programs(1)
    step = pl.program_id(0) * n_k + pl.program_id(1)
    n_steps = pl.num_programs(0) * n_k
    slot = step % FFN_BUFFERS

    def row_copy(tile, buf, j):
        row = idx_sm[tile * SLOT_TILE + j]
        return pltpu.make_async_copy(h_hbm.at[pl.ds(row, 1)], xbuf.at[buf, pl.ds(j, 1)], sem.at[buf])

    def gather(tile, buf):
        def body(j, carry):
            row_copy(tile, buf, j).start()
            return carry
        lax.fori_loop(0, SLOT_TILE, body, 0)

    @pl.when(step == 0)
    def _():
        gather(0, 0)

    @pl.when((step == 0) & (n_steps > 1))
    def _():
        gather(1, 1)

    pltpu.make_async_copy(h_hbm.at[pl.ds(0, SLOT_TILE)], xbuf.at[slot], sem.at[slot]).wait()
    x = xbuf[slot].astype(BF16)

    width = D_FF_EXPERT // FFN_CHUNKS
    acc = None
    for f in range(FFN_CHUNKS):
        cols = slice(f * width, (f + 1) * width)
        hg = jnp.dot(x, wg_ref[:, cols], preferred_element_type=F32)
        hu = jnp.dot(x, wu_ref[:, cols], preferred_element_type=F32)
        he = (hg * _sigmoid(hg) * hu).astype(BF16)
        part = jnp.dot(he, wd_ref[cols, :], preferred_element_type=F32)
        acc = part if acc is None else acc + part
    y_ref[...] = acc * gate_ref[...]

    @pl.when(step + 2 < n_steps)
    def _():
        for j in range(SLOT_TILE):
            row_copy(step + 2, (step + 2) % FFN_BUFFERS, j).start()


def _ffn(idx_flat, h, gate_col, wg_bf16, wu_bf16, wd_bf16, layer, cap):
    n_k = cap // SLOT_TILE
    wspec = lambda shape: pl.BlockSpec((None, None) + shape, lambda e, k, idx: (layer, e, 0, 0))
    tile = lambda e, k, idx: (e * n_k + k, 0)
    return pl.pallas_call(
        _ffn_kernel,
        grid_spec=pltpu.PrefetchScalarGridSpec(
            num_scalar_prefetch=1,
            grid=(N_EXPERTS, n_k),
            in_specs=[pl.BlockSpec(memory_space=pl.ANY),
                      pl.BlockSpec((SLOT_TILE, 1), tile),
                      wspec((D_MODEL, D_FF_EXPERT)), wspec((D_MODEL, D_FF_EXPERT)),
                      wspec((D_FF_EXPERT, D_MODEL))],
            out_specs=pl.BlockSpec((SLOT_TILE, D_MODEL), tile),
            scratch_shapes=[pltpu.VMEM((FFN_BUFFERS, SLOT_TILE, D_MODEL), F32),
                            pltpu.SemaphoreType.DMA((FFN_BUFFERS,))],
        ),
        out_shape=jax.ShapeDtypeStruct((N_EXPERTS * cap, D_MODEL), F32),
        compiler_params=_cparams(("arbitrary", "arbitrary")),
        name="expert_ffn",
    )(idx_flat, h, gate_col, wg_bf16, wu_bf16, wd_bf16)


PIECE = 16
PIECES_PER_EXPERT = -(-(ROUTE_CHUNK + 2 * (V7X_SUBLANES - 1)) // PIECE)
PACK_ROWS = N_EXPERTS * PIECES_PER_EXPERT * PIECE
PACK_BLOCK = 256


def _combine_kernel(off_sm, x_ref, slot_ref, y_hbm, o_ref, buf, sem, *, cap):
    i = pl.program_id(0)
    n_tiles = pl.num_programs(0)
    cur = i % 2

    def plan(tile):
        starts, pieces, bases = [], [], []
        base = 0
        for e in range(N_EXPERTS):
            s0 = off_sm[tile * N_EXPERTS + e]
            s1 = off_sm[(tile + 1) * N_EXPERTS + e]
            a0 = (s0 >> 3) << 3
            rows = jnp.where(s1 > s0, (((s1 + V7X_SUBLANES - 1) >> 3) << 3) - a0, 0)
            n_p = (rows + PIECE - 1) // PIECE
            starts.append(jnp.minimum(a0, cap - n_p * PIECE))
            pieces.append(n_p)
            bases.append(base)
            base = base + n_p * PIECE
        return starts, pieces, bases, base

    def piece_copy(src_row, b, dst_row):
        return pltpu.make_async_copy(y_hbm.at[pl.ds(pl.multiple_of(src_row, V7X_SUBLANES), PIECE)],
                                     buf.at[b, pl.ds(pl.multiple_of(dst_row, V7X_SUBLANES), PIECE)], sem.at[b])

    def fetch(tile, b):
        starts, pieces, bases, _ = plan(tile)
        for e in range(N_EXPERTS):
            def body(j, carry, e=e):
                piece_copy(e * cap + starts[e] + j * PIECE, b, bases[e] + j * PIECE).start()
                return carry
            lax.fori_loop(0, pieces[e], body, 0)

    @pl.when(i == 0)
    def _():
        buf[...] = jnp.zeros_like(buf)
        fetch(0, 0)

    @pl.when(i + 1 < n_tiles)
    def _():
        fetch(i + 1, 1 - cur)

    starts, pieces, bases, total = plan(i)

    def wait_piece(j, carry):
        piece_copy(0, cur, 0).wait()
        return carry
    lax.fori_loop(0, total // PIECE, wait_piece, 0)

    slots = slot_ref[...].astype(F32)
    slots_t = jnp.concatenate([slots, jnp.zeros((ROUTE_CHUNK - N_EXPERTS, ROUTE_CHUNK), F32)], axis=0).T
    packed_row = []
    for e in range(N_EXPERTS):
        col = slots_t[:, e:e + 1]
        shift = (bases[e] - starts[e]).astype(F32)
        packed_row.append(jnp.where(col >= 0.0, col + shift, -1.0))

    o_ref[...] = x_ref[...]
    lane = lax.broadcasted_iota(jnp.int32, (1, PACK_BLOCK), 1).astype(F32)

    def block(kb, carry):
        rows = pl.ds(pl.multiple_of(kb * PACK_BLOCK, PACK_BLOCK), PACK_BLOCK)
        row_id = lane + (kb * PACK_BLOCK).astype(F32)
        onehot = jnp.zeros((ROUTE_CHUNK, PACK_BLOCK), F32)
        for e in range(N_EXPERTS):
            onehot = onehot + jnp.where(packed_row[e] == row_id, 1.0, 0.0)
        onehot = onehot.astype(BF16)
        y = buf[cur, rows, :]
        y_hi = y.astype(BF16)
        y_lo = (y - y_hi.astype(F32)).astype(BF16)
        o_ref[...] = (o_ref[...] + jnp.dot(onehot, y_hi, preferred_element_type=F32)
                      + jnp.dot(onehot, y_lo, preferred_element_type=F32))
        return carry
    lax.fori_loop(0, (total + PACK_BLOCK - 1) // PACK_BLOCK, block, 0)


def _combine(offsets_flat, x, slots, ye, cap):
    n_tok = x.shape[0]
    tok = pl.BlockSpec((ROUTE_CHUNK, D_MODEL), lambda i, off: (i, 0))
    return pl.pallas_call(
        functools.partial(_combine_kernel, cap=cap),
        grid_spec=pltpu.PrefetchScalarGridSpec(
            num_scalar_prefetch=1,
            grid=(n_tok // ROUTE_CHUNK,),
            in_specs=[tok,
                      pl.BlockSpec((None, N_EXPERTS, ROUTE_CHUNK), lambda i, off: (i, 0, 0)),
                      pl.BlockSpec(memory_space=pl.ANY)],
            out_specs=tok,
            scratch_shapes=[pltpu.VMEM((2, PACK_ROWS, D_MODEL), F32), pltpu.SemaphoreType.DMA((2,))],
        ),
        out_shape=jax.ShapeDtypeStruct((n_tok, D_MODEL), F32),
        compiler_params=_cparams(("arbitrary",)),
        name="combine",
    )(offsets_flat, x, slots, ye)


def _final_norm_kernel(x_ref, nw_ref, o_ref):
    x = x_ref[...]
    ms = jnp.mean(x * x, axis=-1, keepdims=True)
    o_ref[...] = x * lax.rsqrt(ms + RMS_EPS) * nw_ref[...]


def _final_norm(x, norm_w):
    n_tok = x.shape[0]
    tm = min(ROW_TILE, n_tok)
    return pl.pallas_call(
        _final_norm_kernel,
        grid=(n_tok // tm,),
        in_specs=[pl.BlockSpec((tm, D_MODEL), lambda i: (i, 0)), pl.BlockSpec((1, D_MODEL), lambda i: (0, 0))],
        out_specs=pl.BlockSpec((tm, D_MODEL), lambda i: (i, 0)),
        out_shape=jax.ShapeDtypeStruct((n_tok, D_MODEL), F32),
        compiler_params=_cparams(("parallel",)),
        name="final_norm",
    )(x, norm_w)


def _moe(x2, h2, aff, params, layer):
    n_tok = x2.shape[0]
    cap = EC_CAPACITY_FACTOR * n_tok // N_EXPERTS
    slots, offs = _select(aff, cap)
    offsets_flat = jnp.concatenate(
        [offs[:, :, 0], jnp.full((1, N_EXPERTS), cap, jnp.int32)], axis=0).reshape(-1)
    idx, gate = _place(offsets_flat, slots, aff, cap)
    ye = _ffn(idx.reshape(-1), h2, gate.reshape(-1, 1), params["w_gate"], params["w_up"], params["w_down"],
              layer, cap)
    return _combine(offsets_flat, x2, slots, ye, cap)


def _trunk(x, params):
    bsz, seq, _ = x.shape
    x = x.reshape(bsz * seq, D_MODEL)
    for layer in range(DEPTH):
        u, q, kf, kb, lf, lb, v, g = _inproj(x, params["norm_mix"], params["w_in"], params["lbf"], params["lbb"], layer)
        yf = _fnet(u, params["w_fnet"], layer, bsz, seq)
        yr = _hgrn(q, kf, kb, lf, lb, v, g, params["hgrn_out_norm"], layer, bsz, seq)
        x2, h2, aff = _outproj(yf, yr, x, params["w_out"], params["norm_ffn"], params["w_router_t"], layer)
        x = _moe(x2, h2, aff, params, layer)
    return _final_norm(x, params["norm_final"]).reshape(bsz, seq, D_MODEL)


def kernel(x_prompt, x_sample, norm_mix, w_in, w_fnet, lb_fwd, lb_bwd, hgrn_out_norm, w_out,
           norm_ffn, w_router, w_gate, w_up, w_down, norm_final):
    params = {
        "norm_mix": norm_mix.reshape(DEPTH, 1, D_MODEL),
        "w_in": w_in.astype(BF16),
        "w_fnet": w_fnet.astype(BF16),
        "lbf": _lower_bounds(lb_fwd).reshape(DEPTH, 1, QK_WIDTH),
        "lbb": _lower_bounds(lb_bwd).reshape(DEPTH, 1, QK_WIDTH),
        "hgrn_out_norm": hgrn_out_norm.reshape(DEPTH, 1, HGRN_DV),
        "w_out": w_out.astype(BF16),
        "norm_ffn": norm_ffn.reshape(DEPTH, 1, D_MODEL),
        "w_router_t": jnp.swapaxes(w_router, 1, 2),
        "w_gate": w_gate.astype(BF16),
        "w_up": w_up.astype(BF16),
        "w_down": w_down.astype(BF16),
        "norm_final": norm_final.reshape(1, D_MODEL),
    }
    return _trunk(x_prompt, params), _trunk(x_sample, params)
```

```python
import functools

import numpy as np
import jax
import jax.numpy as jnp
from jax import lax
from jax.experimental import pallas as pl
from jax.experimental.pallas import tpu as pltpu

D_MODEL = 1024
DEPTH = 4
FNET_WIDTH = 512
FNET_GROUPS = 4
FNET_GROUP_DIM = 128
HGRN_HEADS = 4
HGRN_DK = 128
HGRN_DV = 128
QK_WIDTH = 512
V_WIDTH = 512
IN_WIDTH = FNET_WIDTH + 3 * QK_WIDTH + 2 * V_WIDTH
N_EXPERTS = 16
EC_CAPACITY_FACTOR = 2
D_FF_EXPERT = 2048
RMS_EPS = 1e-6

V7X_LANES = 128
V7X_SUBLANES = 8
V7X_VMEM_LIMIT_BYTES = 56 * 1024 * 1024

BF16 = jnp.bfloat16
F32 = jnp.float32

CHUNK = 64

ROW_TILE = 512
SLOT_TILE = 512
ROUTE_CHUNK = 128


def _cparams(sem, vmem=V7X_VMEM_LIMIT_BYTES):
    return pltpu.CompilerParams(dimension_semantics=sem, vmem_limit_bytes=vmem)


def _sigmoid(x):
    return 1.0 / (1.0 + jnp.exp(-x))


def _lower_bounds_kernel(p_ref, o_ref):
    p = p_ref[...]
    m = jnp.max(p, axis=0, keepdims=True)
    e = jnp.exp(p - m)
    sm = e / jnp.sum(e, axis=0, keepdims=True)
    acc = None
    rows = []
    for layer in range(DEPTH):
        acc = sm[layer:layer + 1] if acc is None else acc + sm[layer:layer + 1]
        rows.append(jnp.maximum(acc - sm[0:1], 0.0))
    o_ref[...] = jnp.concatenate(rows, axis=0)


def _lower_bounds(lb_param):
    return pl.pallas_call(
        _lower_bounds_kernel,
        out_shape=jax.ShapeDtypeStruct((DEPTH, QK_WIDTH), F32),
        name="lower_bounds",
    )(lb_param)


def _hgrn_gates(z, lb):
    e = jnp.exp(-jnp.abs(z))
    r = 1.0 / (1.0 + e)
    sig_neg = jnp.where(z >= 0, e * r, r)
    k = (1.0 - lb) * sig_neg
    log_sig = jnp.minimum(z, 0.0) - jnp.log(1.0 + e)
    c = jnp.log1p(-lb) + log_sig
    has_lb = lb > 0.0
    a = jnp.log(jnp.where(has_lb, lb, 1.0))
    logf = jnp.where(has_lb, jnp.maximum(a, c) + jnp.log(1.0 + jnp.exp(-jnp.abs(a - c))), c)
    return k, logf


def _inproj_kernel(x_ref, nw_ref, w_ref, lbf_ref, lbb_ref,
                   u_ref, q_ref, kf_ref, kb_ref, lf_ref, lbo_ref, v_ref, g_ref):
    x = x_ref[...]
    ms = jnp.mean(x * x, axis=-1, keepdims=True)
    h = (x * lax.rsqrt(ms + RMS_EPS) * nw_ref[...]).astype(BF16)

    def proj(col0, width):
        return jnp.dot(h, w_ref[:, col0:col0 + width], preferred_element_type=F32)

    u_ref[...] = proj(0, FNET_WIDTH)
    q = proj(FNET_WIDTH, QK_WIDTH)
    q_ref[...] = q * _sigmoid(q)
    kf, lf = _hgrn_gates(proj(FNET_WIDTH + QK_WIDTH, QK_WIDTH), lbf_ref[...])
    kf_ref[...] = kf
    lf_ref[...] = lf
    kb, lb = _hgrn_gates(proj(FNET_WIDTH + 2 * QK_WIDTH, QK_WIDTH), lbb_ref[...])
    kb_ref[...] = kb
    lbo_ref[...] = lb
    v_ref[...] = proj(FNET_WIDTH + 3 * QK_WIDTH, V_WIDTH)
    g = proj(FNET_WIDTH + 3 * QK_WIDTH + V_WIDTH, V_WIDTH)
    g_ref[...] = g * _sigmoid(g)


def _inproj(x, norm_w, w_in_bf16, lbf, lbb, layer):
    n_tok = x.shape[0]
    tm = min(ROW_TILE, n_tok)
    row = lambda i: (i, 0)
    lay3 = lambda i: (layer, 0, 0)
    out_sds = jax.ShapeDtypeStruct((n_tok, QK_WIDTH), F32)
    return pl.pallas_call(
        _inproj_kernel,
        grid=(n_tok // tm,),
        in_specs=[
            pl.BlockSpec((tm, D_MODEL), row),
            pl.BlockSpec((None, 1, D_MODEL), lay3),
            pl.BlockSpec((None, D_MODEL, IN_WIDTH), lay3),
            pl.BlockSpec((None, 1, QK_WIDTH), lay3),
            pl.BlockSpec((None, 1, QK_WIDTH), lay3),
        ],
        out_specs=[pl.BlockSpec((tm, QK_WIDTH), row)] * 8,
        out_shape=[out_sds] * 8,
        compiler_params=_cparams(("parallel",)),
        name="inproj",
    )(x, norm_w, w_in_bf16, lbf, lbb)


FNET_RADIX = 4


@functools.lru_cache(maxsize=None)
def _fnet_tables(seq):
    m = seq // FNET_RADIX
    k = np.arange(m, dtype=np.float64)
    ang = 2.0 * np.pi * np.outer(k, k) / m
    cs = np.concatenate([np.cos(ang), np.sin(ang)], axis=0)
    tw = []
    for r in range(1, FNET_RADIX):
        a = 2.0 * np.pi * r * k / seq
        tw.append(np.broadcast_to(np.cos(a)[:, None], (m, V7X_LANES)))
        tw.append(np.broadcast_to(np.sin(a)[:, None], (m, V7X_LANES)))
    tw = np.stack(tw, axis=0)
    c = np.arange(FNET_GROUP_DIM, dtype=np.float64)
    angc = 2.0 * np.pi * np.outer(c, c) / FNET_GROUP_DIM
    scale = 1.0 / np.sqrt(float(seq) * FNET_GROUP_DIM)
    ch = np.concatenate([np.cos(angc), np.sin(angc)], axis=0) * scale
    return (jnp.asarray(cs, dtype=BF16), jnp.asarray(tw, dtype=F32), jnp.asarray(ch, dtype=BF16))


def _fnet_kernel(x0_ref, x1_ref, x2_ref, x3_ref, cs_ref, tw_ref, ch_ref, w_ref, o_ref):
    m = x0_ref.shape[0]
    x = jnp.concatenate([r[...].astype(BF16) for r in (x0_ref, x1_ref, x2_ref, x3_ref)], axis=1)
    pq = jnp.dot(cs_ref[...], x, preferred_element_type=F32)
    h_re, h_im = [], []
    for r in range(FNET_RADIX):
        p = pq[:m, r * V7X_LANES:(r + 1) * V7X_LANES]
        q = pq[m:, r * V7X_LANES:(r + 1) * V7X_LANES]
        if r == 0:
            h_re.append(p)
            h_im.append(-q)
        else:
            tc = tw_ref[2 * (r - 1)]
            ts = tw_ref[2 * (r - 1) + 1]
            h_re.append(tc * p - ts * q)
            h_im.append(-(tc * q + ts * p))
    x_re = [h_re[0] + h_re[1] + h_re[2] + h_re[3],
            h_re[0] + h_im[1] - h_re[2] - h_im[3],
            h_re[0] - h_re[1] + h_re[2] - h_re[3],
            h_re[0] - h_im[1] - h_re[2] + h_im[3]]
    x_im = [h_im[0] + h_im[1] + h_im[2] + h_im[3],
            h_im[0] - h_re[1] - h_im[2] + h_re[3],
            h_im[0] - h_im[1] + h_im[2] - h_im[3],
            h_im[0] + h_re[1] - h_im[2] - h_re[3]]
    xc = jnp.concatenate([jnp.concatenate([a, b], axis=1) for a, b in zip(x_re, x_im)], axis=0)
    mixed = jnp.dot(xc.astype(BF16), ch_ref[...], preferred_element_type=F32)
    o_ref[...] = jnp.dot(mixed.astype(BF16), w_ref[...], preferred_element_type=F32)


def _fnet(u, w_fnet_bf16, layer, bsz, seq):
    m = seq // FNET_RADIX
    cs, tw, ch = _fnet_tables(seq)
    u4 = u.reshape(bsz, m, FNET_RADIX * FNET_WIDTH)
    x_specs = [pl.BlockSpec((None, m, FNET_GROUP_DIM),
                            functools.partial(lambda b, g, r: (b, 0, r * FNET_GROUPS + g), r=r))
               for r in range(FNET_RADIX)]
    out = pl.pallas_call(
        _fnet_kernel,
        grid=(bsz, FNET_GROUPS),
        in_specs=x_specs + [
            pl.BlockSpec((2 * m, m), lambda b, g: (0, 0)),
            pl.BlockSpec((2 * (FNET_RADIX - 1), m, V7X_LANES), lambda b, g: (0, 0, 0)),
            pl.BlockSpec((2 * FNET_GROUP_DIM, FNET_GROUP_DIM), lambda b, g: (0, 0)),
            pl.BlockSpec((None, None, FNET_GROUP_DIM, FNET_GROUP_DIM), lambda b, g: (layer, g, 0, 0)),
        ],
        out_specs=pl.BlockSpec((None, seq, FNET_GROUP_DIM), lambda b, g: (b, 0, g)),
        out_shape=jax.ShapeDtypeStruct((bsz, seq, FNET_WIDTH), F32),
        compiler_params=_cparams(("parallel", "parallel")),
        name="fnet",
    )(u4, u4, u4, u4, cs, tw, ch, w_fnet_bf16)
    return out.reshape(bsz * seq, FNET_WIDTH)


PAIR_LEVELS = ((CHUNK, 16), (16, 4))
PAIR_PARTS = 3
BAND = 4
CHUNK_UNROLL = 8
LOG2_E = 1.4426950408889634


@functools.lru_cache(maxsize=None)
def _hgrn_tables():
    t = np.arange(CHUNK)
    tri = np.stack([(t[None, :] <= t[:, None]), (t[None, :] >= t[:, None])]).astype(np.float32)
    mask = np.zeros((2, len(PAIR_LEVELS), CHUNK, PAIR_PARTS * CHUNK), np.float32)
    for lvl, (outer, inner) in enumerate(PAIR_LEVELS):
        same = (t[:, None] // outer) == (t[None, :] // outer)
        t_in = (t[:, None] % outer) // inner
        s_in = t[None, :] % outer
        for j in range(PAIR_PARTS):
            cols = slice(j * CHUNK, (j + 1) * CHUNK)
            mask[0, lvl, :, cols] = same & (t_in == j + 1) & (s_in < inner * (j + 1))
            mask[1, lvl, :, cols] = same & (t_in == j) & (s_in >= inner * (j + 1))
    return jnp.asarray(tri, dtype=BF16), jnp.asarray(mask)


def _rows_block(p, rows, block):
    def rep(r, n):
        if 0 <= r < CHUNK:
            return jnp.broadcast_to(p[r:r + 1], (n, HGRN_DK))
        return jnp.zeros((n, HGRN_DK), F32)

    if block >= V7X_SUBLANES:
        return jnp.concatenate([rep(r, block) for r in rows], axis=0)
    upper = lax.broadcasted_iota(jnp.int32, (V7X_SUBLANES, HGRN_DK), 0) < block
    return jnp.concatenate([jnp.where(upper, rep(rows[2 * i], V7X_SUBLANES), rep(rows[2 * i + 1], V7X_SUBLANES))
                            for i in range(len(rows) // 2)], axis=0)


def _level_operands(q, k, p, reverse, outer, inner):
    q_rows = [inner * (ib + 1) if reverse else inner * ib - 1 for ib in range(CHUNK // inner)]
    q_l = (q * jnp.exp2(p - _rows_block(p, q_rows, inner))).astype(BF16)
    parts = []
    for j in range(PAIR_PARTS):
        k_rows = [outer * ob + inner * (j + 1) - (0 if reverse else 1) for ob in range(CHUNK // outer)]
        parts.append((k * jnp.exp2(jnp.minimum(_rows_block(p, k_rows, outer) - p, 0.0))).astype(BF16))
    return q_l, jnp.concatenate(parts, axis=0)


def _chunk_rows(c):
    return pl.ds(pl.multiple_of(c * CHUNK, CHUNK), CHUNK)


def _hgrn_prepare(cs, reverse, k_ref, g_ref, v_ref, tri_ref, p_ref, ut_ref, de_ref):
    rows = [_chunk_rows(c) for c in cs]
    g_all = jnp.concatenate([g_ref[r, :] for r in rows], axis=1)
    tri = tri_ref[1 if reverse else 0]
    g_1 = g_all.astype(BF16)
    g_2 = (g_all - g_1.astype(F32)).astype(BF16)
    g_3 = (g_all - g_1.astype(F32) - g_2.astype(F32)).astype(BF16)
    p_all = LOG2_E * (jnp.dot(tri, g_1, preferred_element_type=F32) + jnp.dot(tri, g_2, preferred_element_type=F32)
                      + jnp.dot(tri, g_3, preferred_element_type=F32))
    ps = [p_all[:, i * HGRN_DK:(i + 1) * HGRN_DK] for i in range(len(cs))]
    edges = [p[0:1] if reverse else p[CHUNK - 1:CHUNK] for p in ps]
    kdecs = [(k_ref[r, :] * jnp.exp2(e - p)).astype(BF16) for r, e, p in zip(rows, edges, ps)]
    uts = [lax.dot_general(v_ref[r, :].astype(BF16), kd, (((0,), (0,)), ((), ())), preferred_element_type=F32)
           for r, kd in zip(rows, kdecs)]
    for c, r, p, e, ut in zip(cs, rows, ps, edges, uts):
        p_ref[r, :] = p
        ut_ref[c] = ut
        de_ref[c] = jnp.broadcast_to(jnp.exp2(e), (V7X_SUBLANES, HGRN_DK))


def _hgrn_scan(i, state, reverse, n_chunks, st_ref, ut_ref, de_ref):
    c = n_chunks - 1 - i if reverse else i
    st_ref[c] = state.astype(BF16)
    return state * de_ref[c][0:1] + ut_ref[c]


def _hgrn_output(cs, reverse, q_ref, k_ref, v_ref, p_ref, st_ref, mask_ref, o_ref):
    d = 1 if reverse else 0
    n = len(cs)
    nt = (((1,), (1,)), ((), ()))
    rows = [_chunk_rows(c) for c in cs]
    qs = [q_ref[r, :] for r in rows]
    ks = [k_ref[r, :] for r in rows]
    vs = [v_ref[r, :] for r in rows]
    ps = [p_ref[r, :] for r in rows]
    v_parts = [jnp.concatenate([v.astype(BF16)] * PAIR_PARTS, axis=0) for v in vs]
    outs = [lax.dot_general((qs[i] * jnp.exp2(ps[i])).astype(BF16), st_ref[cs[i]], nt, preferred_element_type=F32)
            for i in range(n)]
    for lvl, (outer, inner) in enumerate(PAIR_LEVELS):
        operands = [_level_operands(qs[i], ks[i], ps[i], reverse, outer, inner) for i in range(n)]
        scores = [lax.dot_general(q_l, k_l, nt, preferred_element_type=F32) for q_l, k_l in operands]
        weights = [(s * mask_ref[d, lvl]).astype(BF16) for s in scores]
        outs = [outs[i] + jnp.dot(weights[i], v_parts[i], preferred_element_type=F32) for i in range(n)]

    t_in = lax.broadcasted_iota(jnp.int32, (CHUNK, 1), 0) % BAND
    for i in range(n):
        q, k, v, p = qs[i], ks[i], vs[i], ps[i]
        out = outs[i] + jnp.sum(q * k, axis=-1, keepdims=True) * v
        for delta in range(1, BAND):
            shift = CHUNK - delta if reverse else delta
            k_s = pltpu.roll(k, shift, 0)
            p_s = pltpu.roll(p, shift, 0)
            v_s = pltpu.roll(v, shift, 0)
            valid = (t_in <= BAND - 1 - delta) if reverse else (t_in >= delta)
            a = jnp.sum(q * k_s * jnp.exp2(jnp.minimum(p - p_s, 0.0)), axis=-1, keepdims=True)
            out = out + jnp.where(valid, a, 0.0) * v_s
        if reverse:
            o_ref[rows[i], :] = o_ref[rows[i], :] + out
        else:
            o_ref[rows[i], :] = out


def _hgrn_kernel(q_ref, kf_ref, kb_ref, lf_ref, lb_ref, v_ref, gate_ref, nw_ref, tri_ref, mask_ref,
                 y_ref, o_ref, p_ref, ut_ref, st_ref, de_ref):
    seq = q_ref.shape[0]
    n_chunks = seq // CHUNK

    def per_chunk(fn):
        def body(i, carry):
            fn([i * CHUNK_UNROLL + u for u in range(CHUNK_UNROLL)])
            return carry
        lax.fori_loop(0, n_chunks // CHUNK_UNROLL, body, 0)

    for reverse, k_ref, g_ref in ((False, kf_ref, lf_ref), (True, kb_ref, lb_ref)):
        per_chunk(functools.partial(_hgrn_prepare, reverse=reverse, k_ref=k_ref, g_ref=g_ref, v_ref=v_ref,
                                    tri_ref=tri_ref, p_ref=p_ref, ut_ref=ut_ref, de_ref=de_ref))
        lax.fori_loop(0, n_chunks,
                      functools.partial(_hgrn_scan, reverse=reverse, n_chunks=n_chunks, st_ref=st_ref,
                                        ut_ref=ut_ref, de_ref=de_ref),
                      jnp.zeros((HGRN_DV, HGRN_DK), F32))
        per_chunk(functools.partial(_hgrn_output, reverse=reverse, q_ref=q_ref, k_ref=k_ref, v_ref=v_ref,
                                    p_ref=p_ref, st_ref=st_ref, mask_ref=mask_ref, o_ref=o_ref))
    o = o_ref[...]
    ms = jnp.mean(o * o, axis=-1, keepdims=True)
    y_ref[...] = o * lax.rsqrt(ms + RMS_EPS) * nw_ref[...] * gate_ref[...]


def _hgrn(q, kf, kb, lf, lb, v, gate, out_norm, layer, bsz, seq):
    tri, mask = _hgrn_tables()
    n_chunks = seq // CHUNK
    as3 = lambda a: a.reshape(bsz, seq, QK_WIDTH)
    head = pl.BlockSpec((None, seq, HGRN_DK), lambda b, h: (b, 0, h))
    out = pl.pallas_call(
        _hgrn_kernel,
        grid=(bsz, HGRN_HEADS),
        in_specs=[head] * 7 + [
            pl.BlockSpec((None, 1, HGRN_DV), lambda b, h: (layer, 0, 0)),
            pl.BlockSpec(tri.shape, lambda b, h: (0, 0, 0)),
            pl.BlockSpec(mask.shape, lambda b, h: (0, 0, 0, 0)),
        ],
        out_specs=head,
        out_shape=jax.ShapeDtypeStruct((bsz, seq, V_WIDTH), F32),
        scratch_shapes=[
            pltpu.VMEM((seq, HGRN_DV), F32),
            pltpu.VMEM((seq, HGRN_DK), F32),
            pltpu.VMEM((n_chunks, HGRN_DV, HGRN_DK), F32),
            pltpu.VMEM((n_chunks, HGRN_DV, HGRN_DK), BF16),
            pltpu.VMEM((n_chunks, V7X_SUBLANES, HGRN_DK), F32),
        ],
        compiler_params=_cparams(("parallel", "parallel")),
        name="hgrn",
    )(as3(q), as3(kf), as3(kb), as3(lf), as3(lb), as3(v), as3(gate), out_norm, tri, mask)
    return out.reshape(bsz * seq, V_WIDTH)


def _outproj_kernel(yf_ref, yr_ref, x_ref, w_ref, nw_ref, wr_ref, x2_ref, h2_ref, aff_ref):
    x2 = (x_ref[...]
          + jnp.dot(yf_ref[...].astype(BF16), w_ref[:FNET_WIDTH, :], preferred_element_type=F32)
          + jnp.dot(yr_ref[...].astype(BF16), w_ref[FNET_WIDTH:, :], preferred_element_type=F32))
    x2_ref[...] = x2
    ms = jnp.mean(x2 * x2, axis=-1, keepdims=True)
    h2 = x2 * lax.rsqrt(ms + RMS_EPS) * nw_ref[...]
    h2_ref[...] = h2
    nt = (((1,), (1,)), ((), ()))
    h_hi = h2.astype(BF16)
    h_lo = (h2 - h_hi.astype(F32)).astype(BF16)
    w = wr_ref[...]
    w_hi = w.astype(BF16)
    w_lo = (w - w_hi.astype(F32)).astype(BF16)
    both = lax.dot_general(jnp.concatenate([w_hi, w_lo], axis=0), h_hi, nt, preferred_element_type=F32)
    logits = (both[:N_EXPERTS] + both[N_EXPERTS:]
              + lax.dot_general(w_hi, h_lo, nt, preferred_element_type=F32))
    e = jnp.exp(logits - jnp.max(logits, axis=0, keepdims=True))
    aff = e / jnp.sum(e, axis=0, keepdims=True)
    for j in range(aff_ref.shape[0]):
        aff_ref[j] = aff[:, j * ROUTE_CHUNK:(j + 1) * ROUTE_CHUNK]


def _outproj(yf, yr, x, w_out_bf16, norm_w, w_router_t, layer):
    n_tok = x.shape[0]
    tm = min(ROW_TILE, n_tok)
    row = lambda i: (i, 0)
    lay3 = lambda i: (layer, 0, 0)
    return pl.pallas_call(
        _outproj_kernel,
        grid=(n_tok // tm,),
        in_specs=[
            pl.BlockSpec((tm, FNET_WIDTH), row),
            pl.BlockSpec((tm, V_WIDTH), row),
            pl.BlockSpec((tm, D_MODEL), row),
            pl.BlockSpec((None, D_MODEL, D_MODEL), lay3),
            pl.BlockSpec((None, 1, D_MODEL), lay3),
            pl.BlockSpec((None, N_EXPERTS, D_MODEL), lay3),
        ],
        out_specs=[
            pl.BlockSpec((tm, D_MODEL), row),
            pl.BlockSpec((tm, D_MODEL), row),
            pl.BlockSpec((tm // ROUTE_CHUNK, N_EXPERTS, ROUTE_CHUNK), lambda i: (i, 0, 0)),
        ],
        out_shape=[
            jax.ShapeDtypeStruct((n_tok, D_MODEL), F32),
            jax.ShapeDtypeStruct((n_tok, D_MODEL), F32),
            jax.ShapeDtypeStruct((n_tok // ROUTE_CHUNK, N_EXPERTS, ROUTE_CHUNK), F32),
        ],
        compiler_params=_cparams(("parallel",)),
        name="outproj",
    )(yf, yr, x, w_out_bf16, norm_w, w_router_t)


ONE_F32_BITS = 0x3F800000
BISECT_STEPS = 31


def _select_kernel(aff_ref, tri_ref, slot_ref, off_ref, *, cap):
    n_chunks = aff_ref.shape[0]
    bits = pltpu.bitcast(aff_ref[...], jnp.int32)

    def count(pred):
        return jnp.sum(jnp.sum(pred.astype(F32), axis=0), axis=1, keepdims=True)

    def bisect(_, lo_hi):
        lo, hi = lo_hi
        mid = lo + ((hi - lo + 1) >> 1)
        ok = count(bits >= mid[None]) >= cap
        return jnp.where(ok, mid, lo), jnp.where(ok, hi, mid - 1)

    cut, _ = lax.fori_loop(0, BISECT_STEPS, bisect,
                           (jnp.zeros((N_EXPERTS, 1), jnp.int32),
                            jnp.full((N_EXPERTS, 1), ONE_F32_BITS, jnp.int32)))
    ties_wanted = cap - count(bits > cut[None])

    def scan(c, carry):
        n_sel, n_tie = carry
        b = pltpu.bitcast(aff_ref[c], jnp.int32)
        above = b > cut
        tie = (b == cut).astype(F32)
        tie_rank = n_tie + jnp.dot(tie.astype(BF16), tri_ref[...], preferred_element_type=F32) - tie
        sel = jnp.where(above, 1.0, jnp.where(tie_rank < ties_wanted, tie, 0.0))
        slot = n_sel + jnp.dot(sel.astype(BF16), tri_ref[...], preferred_element_type=F32) - sel
        slot_ref[c] = jnp.where(sel > 0.0, slot, -1.0).astype(jnp.int32)
        off_ref[c] = jnp.broadcast_to(n_sel, (N_EXPERTS, ROUTE_CHUNK)).astype(jnp.int32)
        return (n_sel + jnp.sum(sel, axis=1, keepdims=True), n_tie + jnp.sum(tie, axis=1, keepdims=True))

    zero = jnp.zeros((N_EXPERTS, 1), F32)
    lax.fori_loop(0, n_chunks, scan, (zero, zero))


@functools.lru_cache(maxsize=None)
def _prefix_table():
    t = np.arange(ROUTE_CHUNK)
    return jnp.asarray((t[:, None] <= t[None, :]).astype(np.float32), dtype=BF16)


def _select(aff, cap):
    n_chunks = aff.shape[0]
    sds = jax.ShapeDtypeStruct((n_chunks, N_EXPERTS, ROUTE_CHUNK), jnp.int32)
    return pl.pallas_call(
        functools.partial(_select_kernel, cap=cap),
        out_shape=[sds, sds],
        compiler_params=pltpu.CompilerParams(vmem_limit_bytes=V7X_VMEM_LIMIT_BYTES),
        name="route_select",
    )(aff, _prefix_table())


PLACE_ROWS = 8


def _place_kernel(off_sm, slot_ref, aff_ref, idx_ref, gate_ref, acc_ref):
    n_chunks = slot_ref.shape[0]
    n_tiles = idx_ref.shape[1]
    acc_ref[...] = jnp.zeros_like(acc_ref)
    lane = lax.broadcasted_iota(jnp.int32, (1, ROUTE_CHUNK), 1)
    j_iota = lax.broadcasted_iota(jnp.int32, (ROUTE_CHUNK, ROUTE_CHUNK), 0)
    lane8 = lax.broadcasted_iota(jnp.int32, (PLACE_ROWS, ROUTE_CHUNK), 1)
    lo_row = lane.astype(F32)

    def chunk(c, carry):
        slots = slot_ref[c]
        aff = aff_ref[c]
        hi_row = jnp.full((1, ROUTE_CHUNK), c, jnp.int32).astype(F32)
        for e in range(N_EXPERTS):
            off = off_sm[c * N_EXPERTS + e]
            tile = off >> 7
            shift = off & (ROUTE_CHUNK - 1)
            local = slots[e:e + 1] - off
            onehot = jnp.where(local == j_iota, 1.0, 0.0).astype(BF16)
            a = aff[e:e + 1]
            g1 = a.astype(BF16)
            g2 = (a - g1.astype(F32)).astype(BF16)
            g3 = (a - g1.astype(F32) - g2.astype(F32)).astype(BF16)
            lhs = jnp.concatenate([hi_row.astype(BF16), lo_row.astype(BF16), g1, g2, g3,
                                   jnp.zeros((PLACE_ROWS - 5, ROUTE_CHUNK), BF16)], axis=0)
            packed = lax.dot_general(lhs, onehot, (((1,), (1,)), ((), ())), preferred_element_type=F32)
            rolled = pltpu.roll(packed, shift, 1)
            acc_ref[e, tile] = acc_ref[e, tile] + jnp.where(lane8 >= shift, rolled, 0.0)
            acc_ref[e, tile + 1] = acc_ref[e, tile + 1] + jnp.where(lane8 < shift, rolled, 0.0)
        return carry

    lax.fori_loop(0, n_chunks, chunk, 0)
    for e in range(N_EXPERTS):
        hi = acc_ref[e, :, 0, :]
        lo = acc_ref[e, :, 1, :]
        idx_ref[e] = (hi * float(ROUTE_CHUNK) + lo).astype(jnp.int32)[:n_tiles]
        gate_ref[e] = (acc_ref[e, :, 2, :] + acc_ref[e, :, 3, :] + acc_ref[e, :, 4, :])[:n_tiles]


def _place(offsets_flat, slots, aff, cap):
    n_tiles = cap // ROUTE_CHUNK
    return pl.pallas_call(
        _place_kernel,
        grid_spec=pltpu.PrefetchScalarGridSpec(
            num_scalar_prefetch=1,
            grid=(1,),
            in_specs=[pl.BlockSpec(slots.shape, lambda i, off: (0, 0, 0)),
                      pl.BlockSpec(aff.shape, lambda i, off: (0, 0, 0))],
            out_specs=[pl.BlockSpec((N_EXPERTS, n_tiles, ROUTE_CHUNK), lambda i, off: (0, 0, 0))] * 2,
            scratch_shapes=[pltpu.VMEM((N_EXPERTS, n_tiles + 2, PLACE_ROWS, ROUTE_CHUNK), F32)],
        ),
        out_shape=[jax.ShapeDtypeStruct((N_EXPERTS, n_tiles, ROUTE_CHUNK), jnp.int32),
                   jax.ShapeDtypeStruct((N_EXPERTS, n_tiles, ROUTE_CHUNK), F32)],
        compiler_params=_cparams(("arbitrary",)),
        name="route_place",
    )(offsets_flat, slots, aff)


FFN_CHUNKS = 8
FFN_BUFFERS = 3


def _ffn_kernel(idx_sm, h_hbm, gate_ref, wg_ref, wu_ref, wd_ref, y_ref, xbuf, sem):
    n_k = pl.num_programs(1)
    step = pl.program_id(0) * n_k + pl.program_id(1)
    n_steps = pl.num_programs(0) * n_k
    slot = step % FFN_BUFFERS

    def row_copy(tile, buf, j):
        row = idx_sm[tile * SLOT_TILE + j]
        return pltpu.make_async_copy(h_hbm.at[pl.ds(row, 1)], xbuf.at[buf, pl.ds(j, 1)], sem.at[buf])

    def gather(tile, buf):
        def body(j, carry):
            row_copy(tile, buf, j).start()
            return carry
        lax.fori_loop(0, SLOT_TILE, body, 0)

    @pl.when(step == 0)
    def _():
        gather(0, 0)

    @pl.when((step == 0) & (n_steps > 1))
    def _():
        gather(1, 1)

    pltpu.make_async_copy(h_hbm.at[pl.ds(0, SLOT_TILE)], xbuf.at[slot], sem.at[slot]).wait()
    x = xbuf[slot].astype(BF16)

    width = D_FF_EXPERT // FFN_CHUNKS
    acc = None
    for f in range(FFN_CHUNKS):
        cols = slice(f * width, (f + 1) * width)
        hg = jnp.dot(x, wg_ref[:, cols], preferred_element_type=F32)
        hu = jnp.dot(x, wu_ref[:, cols], preferred_element_type=F32)
        he = (hg * _sigmoid(hg) * hu).astype(BF16)
        part = jnp.dot(he, wd_ref[cols, :], preferred_element_type=F32)
        acc = part if acc is None else acc + part
    y_ref[...] = acc * gate_ref[...]

    @pl.when(step + 2 < n_steps)
    def _():
        for j in range(SLOT_TILE):
            row_copy(step + 2, (step + 2) % FFN_BUFFERS, j).start()


def _ffn(idx_flat, h, gate_col, wg_bf16, wu_bf16, wd_bf16, layer, cap):
    assert cap % SLOT_TILE == 0
    n_k = cap // SLOT_TILE
    wspec = lambda shape: pl.BlockSpec((None, None) + shape, lambda e, k, idx: (layer, e, 0, 0))
    tile = lambda e, k, idx: (e * n_k + k, 0)
    return pl.pallas_call(
        _ffn_kernel,
        grid_spec=pltpu.PrefetchScalarGridSpec(
            num_scalar_prefetch=1,
            grid=(N_EXPERTS, n_k),
            in_specs=[pl.BlockSpec(memory_space=pl.ANY),
                      pl.BlockSpec((SLOT_TILE, 1), tile),
                      wspec((D_MODEL, D_FF_EXPERT)), wspec((D_MODEL, D_FF_EXPERT)),
                      wspec((D_FF_EXPERT, D_MODEL))],
            out_specs=pl.BlockSpec((SLOT_TILE, D_MODEL), tile),
            scratch_shapes=[pltpu.VMEM((FFN_BUFFERS, SLOT_TILE, D_MODEL), F32),
                            pltpu.SemaphoreType.DMA((FFN_BUFFERS,))],
        ),
        out_shape=jax.ShapeDtypeStruct((N_EXPERTS * cap, D_MODEL), F32),
        compiler_params=_cparams(("arbitrary", "arbitrary")),
        name="expert_ffn",
    )(idx_flat, h, gate_col, wg_bf16, wu_bf16, wd_bf16)


PIECE = 32
SLOT_SPLIT_BITS = 6
SLOT_SPLIT = 1 << SLOT_SPLIT_BITS
PIECES_PER_EXPERT = -(-(ROUTE_CHUNK + 2 * (V7X_SUBLANES - 1)) // PIECE)
PACK_ROWS = N_EXPERTS * PIECES_PER_EXPERT * PIECE
PACK_BLOCK = 256


def _combine_kernel(off_sm, x_ref, slot_ref, y_hbm, o_ref, buf, sem, *, cap):
    i = pl.program_id(0)
    n_tiles = pl.num_programs(0)
    cur = i % 2

    def plan(tile):
        starts, pieces, bases = [], [], []
        base = 0
        for e in range(N_EXPERTS):
            s0 = off_sm[tile * N_EXPERTS + e]
            s1 = off_sm[(tile + 1) * N_EXPERTS + e]
            a0 = (s0 >> 3) << 3
            rows = jnp.where(s1 > s0, (((s1 + V7X_SUBLANES - 1) >> 3) << 3) - a0, 0)
            n_p = (rows + PIECE - 1) // PIECE
            starts.append(jnp.minimum(a0, cap - n_p * PIECE))
            pieces.append(n_p)
            bases.append(base)
            base = base + n_p * PIECE
        return starts, pieces, bases, base

    def piece_copy(src_row, b, dst_row):
        return pltpu.make_async_copy(y_hbm.at[pl.ds(pl.multiple_of(src_row, V7X_SUBLANES), PIECE)],
                                     buf.at[b, pl.ds(pl.multiple_of(dst_row, V7X_SUBLANES), PIECE)], sem.at[b])

    def fetch(tile, b):
        starts, pieces, bases, _ = plan(tile)
        for e in range(N_EXPERTS):
            def body(j, carry, e=e):
                piece_copy(e * cap + starts[e] + j * PIECE, b, bases[e] + j * PIECE).start()
                return carry
            lax.fori_loop(0, pieces[e], body, 0)

    @pl.when(i == 0)
    def _():
        buf[...] = jnp.zeros_like(buf)
        fetch(0, 0)

    @pl.when(i + 1 < n_tiles)
    def _():
        fetch(i + 1, 1 - cur)

    starts, pieces, bases, total = plan(i)

    def wait_piece(j, carry):
        piece_copy(0, cur, 0).wait()
        return carry
    lax.fori_loop(0, total // PIECE, wait_piece, 0)

    slot1 = slot_ref[...] + 1
    parts = jnp.concatenate([(slot1 >> SLOT_SPLIT_BITS).astype(F32), (slot1 & (SLOT_SPLIT - 1)).astype(F32),
                             jnp.zeros((ROUTE_CHUNK - 2 * N_EXPERTS, ROUTE_CHUNK), F32)], axis=0)
    slot_parts = parts.T[:, :2 * N_EXPERTS].astype(BF16)
    e_iota = lax.broadcasted_iota(jnp.int32, (N_EXPERTS, 1), 0)
    first = jnp.zeros((N_EXPERTS, 1), F32)
    last = jnp.zeros((N_EXPERTS, 1), F32)
    shift = jnp.zeros((N_EXPERTS, 1), F32)
    as_f32 = lambda v: jnp.asarray(v, jnp.int32).astype(F32)
    for e in range(N_EXPERTS):
        here = e_iota == e
        first = jnp.where(here, as_f32(bases[e]), first)
        last = jnp.where(here, as_f32(bases[e] + pieces[e] * PIECE), last)
        shift = jnp.where(here, as_f32(bases[e] - starts[e]), shift)

    o_ref[...] = x_ref[...]
    lane = lax.broadcasted_iota(jnp.int32, (1, PACK_BLOCK), 1).astype(F32)

    def block(kb, carry):
        rows = pl.ds(pl.multiple_of(kb * PACK_BLOCK, PACK_BLOCK), PACK_BLOCK)
        row_id = lane + (kb * PACK_BLOCK).astype(F32)
        member = jnp.where((row_id >= first) & (row_id < last), 1.0, 0.0)
        spread = jnp.concatenate([member * float(SLOT_SPLIT), member], axis=0).astype(BF16)
        slot_of_row = jnp.dot(slot_parts, spread, preferred_element_type=F32)
        wanted = row_id - jnp.sum(member * shift, axis=0, keepdims=True) + 1.0
        onehot = jnp.where(slot_of_row == wanted, 1.0, 0.0).astype(BF16)
        y = buf[cur, rows, :]
        y_hi = y.astype(BF16)
        y_lo = (y - y_hi.astype(F32)).astype(BF16)
        o_ref[...] = (o_ref[...] + jnp.dot(onehot, y_hi, preferred_element_type=F32)
                      + jnp.dot(onehot, y_lo, preferred_element_type=F32))
        return carry
    lax.fori_loop(0, (total + PACK_BLOCK - 1) // PACK_BLOCK, block, 0)


def _combine(offsets_flat, x, slots, ye, cap):
    assert cap < SLOT_SPLIT * 256 and cap % PIECE == 0
    n_tok = x.shape[0]
    tok = pl.BlockSpec((ROUTE_CHUNK, D_MODEL), lambda i, off: (i, 0))
    return pl.pallas_call(
        functools.partial(_combine_kernel, cap=cap),
        grid_spec=pltpu.PrefetchScalarGridSpec(
            num_scalar_prefetch=1,
            grid=(n_tok // ROUTE_CHUNK,),
            in_specs=[tok,
                      pl.BlockSpec((None, N_EXPERTS, ROUTE_CHUNK), lambda i, off: (i, 0, 0)),
                      pl.BlockSpec(memory_space=pl.ANY)],
            out_specs=tok,
            scratch_shapes=[pltpu.VMEM((2, PACK_ROWS, D_MODEL), F32), pltpu.SemaphoreType.DMA((2,))],
        ),
        out_shape=jax.ShapeDtypeStruct((n_tok, D_MODEL), F32),
        compiler_params=_cparams(("arbitrary",)),
        name="combine",
    )(offsets_flat, x, slots, ye)


def _final_norm_kernel(x_ref, nw_ref, o_ref):
    x = x_ref[...]
    ms = jnp.mean(x * x, axis=-1, keepdims=True)
    o_ref[...] = x * lax.rsqrt(ms + RMS_EPS) * nw_ref[...]


def _final_norm(x, norm_w):
    n_tok = x.shape[0]
    tm = min(ROW_TILE, n_tok)
    return pl.pallas_call(
        _final_norm_kernel,
        grid=(n_tok // tm,),
        in_specs=[pl.BlockSpec((tm, D_MODEL), lambda i: (i, 0)), pl.BlockSpec((1, D_MODEL), lambda i: (0, 0))],
        out_specs=pl.BlockSpec((tm, D_MODEL), lambda i: (i, 0)),
        out_shape=jax.ShapeDtypeStruct((n_tok, D_MODEL), F32),
        compiler_params=_cparams(("parallel",)),
        name="final_norm",
    )(x, norm_w)


def _moe(x2, h2, aff, params, layer):
    n_tok = x2.shape[0]
    cap = EC_CAPACITY_FACTOR * n_tok // N_EXPERTS
    slots, offs = _select(aff, cap)
    offsets_flat = jnp.concatenate(
        [offs[:, :, 0], jnp.full((1, N_EXPERTS), cap, jnp.int32)], axis=0).reshape(-1)
    idx, gate = _place(offsets_flat, slots, aff, cap)
    ye = _ffn(idx.reshape(-1), h2, gate.reshape(-1, 1), params["w_gate"], params["w_up"], params["w_down"],
              layer, cap)
    return _combine(offsets_flat, x2, slots, ye, cap)


def _trunk(x, params):
    bsz, seq, _ = x.shape
    x = x.reshape(bsz * seq, D_MODEL)
    for layer in range(DEPTH):
        u, q, kf, kb, lf, lb, v, g = _inproj(x, params["norm_mix"], params["w_in"], params["lbf"], params["lbb"], layer)
        yf = _fnet(u, params["w_fnet"], layer, bsz, seq)
        yr = _hgrn(q, kf, kb, lf, lb, v, g, params["hgrn_out_norm"], layer, bsz, seq)
        x2, h2, aff = _outproj(yf, yr, x, params["w_out"], params["norm_ffn"], params["w_router_t"], layer)
        x = _moe(x2, h2, aff, params, layer)
    return _final_norm(x, params["norm_final"]).reshape(bsz, seq, D_MODEL)


def kernel(x_prompt, x_sample, norm_mix, w_in, w_fnet, lb_fwd, lb_bwd, hgrn_out_norm, w_out,
           norm_ffn, w_router, w_gate, w_up, w_down, norm_final):
    params = {
        "norm_mix": norm_mix.reshape(DEPTH, 1, D_MODEL),
        "w_in": w_in.astype(BF16),
        "w_fnet": w_fnet.astype(BF16),
        "lbf": _lower_bounds(lb_fwd).reshape(DEPTH, 1, QK_WIDTH),
        "lbb": _lower_bounds(lb_bwd).reshape(DEPTH, 1, QK_WIDTH),
        "hgrn_out_norm": hgrn_out_norm.reshape(DEPTH, 1, HGRN_DV),
        "w_out": w_out.astype(BF16),
        "norm_ffn": norm_ffn.reshape(DEPTH, 1, D_MODEL),
        "w_router_t": jnp.swapaxes(w_router, 1, 2),
        "w_gate": w_gate.astype(BF16),
        "w_up": w_up.astype(BF16),
        "w_down": w_down.astype(BF16),
        "norm_final": norm_final.reshape(1, D_MODEL),
    }
    return _trunk(x_prompt, params), _trunk(x_sample, params)
```

```python
import functools

import numpy as np
import jax
import jax.numpy as jnp
from jax import lax
from jax.experimental import pallas as pl
from jax.experimental.pallas import tpu as pltpu

D_MODEL = 1024
DEPTH = 4
FNET_WIDTH = 512
FNET_GROUPS = 4
FNET_GROUP_DIM = 128
HGRN_HEADS = 4
HGRN_DK = 128
HGRN_DV = 128
QK_WIDTH = 512
V_WIDTH = 512
IN_WIDTH = FNET_WIDTH + 3 * QK_WIDTH + 2 * V_WIDTH
N_EXPERTS = 16
EC_CAPACITY_FACTOR = 2
D_FF_EXPERT = 2048
RMS_EPS = 1e-6

V7X_LANES = 128
V7X_SUBLANES = 8
V7X_VMEM_LIMIT_BYTES = 56 * 1024 * 1024

BF16 = jnp.bfloat16
F32 = jnp.float32

CHUNK = 64

ROW_TILE = 512
SLOT_TILE = 512
ROUTE_CHUNK = 128


def _cparams(sem, vmem=V7X_VMEM_LIMIT_BYTES):
    return pltpu.CompilerParams(dimension_semantics=sem, vmem_limit_bytes=vmem)


def _sigmoid(x):
    return 1.0 / (1.0 + jnp.exp(-x))


def _lower_bounds_kernel(p_ref, o_ref):
    p = p_ref[...]
    m = jnp.max(p, axis=0, keepdims=True)
    e = jnp.exp(p - m)
    sm = e / jnp.sum(e, axis=0, keepdims=True)
    acc = None
    rows = []
    for layer in range(DEPTH):
        acc = sm[layer:layer + 1] if acc is None else acc + sm[layer:layer + 1]
        rows.append(jnp.maximum(acc - sm[0:1], 0.0))
    o_ref[...] = jnp.concatenate(rows, axis=0)


def _lower_bounds(lb_param):
    return pl.pallas_call(
        _lower_bounds_kernel,
        out_shape=jax.ShapeDtypeStruct((DEPTH, QK_WIDTH), F32),
        name="lower_bounds",
    )(lb_param)


def _hgrn_gates(z, lb):
    e = jnp.exp(-jnp.abs(z))
    r = 1.0 / (1.0 + e)
    sig_neg = jnp.where(z >= 0, e * r, r)
    k = (1.0 - lb) * sig_neg
    log_sig = jnp.minimum(z, 0.0) - jnp.log(1.0 + e)
    c = jnp.log1p(-lb) + log_sig
    has_lb = lb > 0.0
    a = jnp.log(jnp.where(has_lb, lb, 1.0))
    logf = jnp.where(has_lb, jnp.maximum(a, c) + jnp.log(1.0 + jnp.exp(-jnp.abs(a - c))), c)
    return k, logf


def _inproj_kernel(x_ref, nw_ref, w_ref, lbf_ref, lbb_ref,
                   u_ref, q_ref, kf_ref, kb_ref, lf_ref, lbo_ref, v_ref, g_ref):
    x = x_ref[...]
    ms = jnp.mean(x * x, axis=-1, keepdims=True)
    h = (x * lax.rsqrt(ms + RMS_EPS) * nw_ref[...]).astype(BF16)

    def proj(col0, width):
        return jnp.dot(h, w_ref[:, col0:col0 + width], preferred_element_type=F32)

    u_ref[...] = proj(0, FNET_WIDTH)
    q = proj(FNET_WIDTH, QK_WIDTH)
    q_ref[...] = q * _sigmoid(q)
    kf, lf = _hgrn_gates(proj(FNET_WIDTH + QK_WIDTH, QK_WIDTH), lbf_ref[...])
    kf_ref[...] = kf
    lf_ref[...] = lf
    kb, lb = _hgrn_gates(proj(FNET_WIDTH + 2 * QK_WIDTH, QK_WIDTH), lbb_ref[...])
    kb_ref[...] = kb
    lbo_ref[...] = lb
    v_ref[...] = proj(FNET_WIDTH + 3 * QK_WIDTH, V_WIDTH)
    g = proj(FNET_WIDTH + 3 * QK_WIDTH + V_WIDTH, V_WIDTH)
    g_ref[...] = g * _sigmoid(g)


def _inproj(x, norm_w, w_in_bf16, lbf, lbb, layer):
    n_tok = x.shape[0]
    tm = min(ROW_TILE, n_tok)
    row = lambda i: (i, 0)
    lay3 = lambda i: (layer, 0, 0)
    out_sds = jax.ShapeDtypeStruct((n_tok, QK_WIDTH), F32)
    return pl.pallas_call(
        _inproj_kernel,
        grid=(n_tok // tm,),
        in_specs=[
            pl.BlockSpec((tm, D_MODEL), row),
            pl.BlockSpec((None, 1, D_MODEL), lay3),
            pl.BlockSpec((None, D_MODEL, IN_WIDTH), lay3),
            pl.BlockSpec((None, 1, QK_WIDTH), lay3),
            pl.BlockSpec((None, 1, QK_WIDTH), lay3),
        ],
        out_specs=[pl.BlockSpec((tm, QK_WIDTH), row)] * 8,
        out_shape=[out_sds] * 8,
        compiler_params=_cparams(("parallel",)),
        name="inproj",
    )(x, norm_w, w_in_bf16, lbf, lbb)


FNET_RADIX = 4


@functools.lru_cache(maxsize=None)
def _fnet_tables(seq):
    m = seq // FNET_RADIX
    k = np.arange(m, dtype=np.float64)
    ang = 2.0 * np.pi * np.outer(k, k) / m
    cs = np.concatenate([np.cos(ang), np.sin(ang)], axis=0)
    tw = []
    for r in range(1, FNET_RADIX):
        a = 2.0 * np.pi * r * k / seq
        tw.append(np.broadcast_to(np.cos(a)[:, None], (m, V7X_LANES)))
        tw.append(np.broadcast_to(np.sin(a)[:, None], (m, V7X_LANES)))
    tw = np.stack(tw, axis=0)
    c = np.arange(FNET_GROUP_DIM, dtype=np.float64)
    angc = 2.0 * np.pi * np.outer(c, c) / FNET_GROUP_DIM
    scale = 1.0 / np.sqrt(float(seq) * FNET_GROUP_DIM)
    ch = np.concatenate([np.cos(angc), np.sin(angc)], axis=0) * scale
    return (jnp.asarray(cs, dtype=BF16), jnp.asarray(tw, dtype=F32), jnp.asarray(ch, dtype=BF16))


def _fnet_kernel(x0_ref, x1_ref, x2_ref, x3_ref, cs_ref, tw_ref, ch_ref, w_ref, o_ref):
    m = x0_ref.shape[0]
    x = jnp.concatenate([r[...].astype(BF16) for r in (x0_ref, x1_ref, x2_ref, x3_ref)], axis=1)
    pq = jnp.dot(cs_ref[...], x, preferred_element_type=F32)
    h_re, h_im = [], []
    for r in range(FNET_RADIX):
        p = pq[:m, r * V7X_LANES:(r + 1) * V7X_LANES]
        q = pq[m:, r * V7X_LANES:(r + 1) * V7X_LANES]
        if r == 0:
            h_re.append(p)
            h_im.append(-q)
        else:
            tc = tw_ref[2 * (r - 1)]
            ts = tw_ref[2 * (r - 1) + 1]
            h_re.append(tc * p - ts * q)
            h_im.append(-(tc * q + ts * p))
    x_re = [h_re[0] + h_re[1] + h_re[2] + h_re[3],
            h_re[0] + h_im[1] - h_re[2] - h_im[3],
            h_re[0] - h_re[1] + h_re[2] - h_re[3],
            h_re[0] - h_im[1] - h_re[2] + h_im[3]]
    x_im = [h_im[0] + h_im[1] + h_im[2] + h_im[3],
            h_im[0] - h_re[1] - h_im[2] + h_re[3],
            h_im[0] - h_im[1] + h_im[2] - h_im[3],
            h_im[0] + h_re[1] - h_im[2] - h_re[3]]
    xc = jnp.concatenate([jnp.concatenate([a, b], axis=1) for a, b in zip(x_re, x_im)], axis=0)
    mixed = jnp.dot(xc.astype(BF16), ch_ref[...], preferred_element_type=F32)
    o_ref[...] = jnp.dot(mixed.astype(BF16), w_ref[...], preferred_element_type=F32)


def _fnet(u, w_fnet_bf16, layer, bsz, seq):
    m = seq // FNET_RADIX
    cs, tw, ch = _fnet_tables(seq)
    u4 = u.reshape(bsz, m, FNET_RADIX * FNET_WIDTH)
    x_specs = [pl.BlockSpec((None, m, FNET_GROUP_DIM),
                            functools.partial(lambda b, g, r: (b, 0, r * FNET_GROUPS + g), r=r))
               for r in range(FNET_RADIX)]
    out = pl.pallas_call(
        _fnet_kernel,
        grid=(bsz, FNET_GROUPS),
        in_specs=x_specs + [
            pl.BlockSpec((2 * m, m), lambda b, g: (0, 0)),
            pl.BlockSpec((2 * (FNET_RADIX - 1), m, V7X_LANES), lambda b, g: (0, 0, 0)),
            pl.BlockSpec((2 * FNET_GROUP_DIM, FNET_GROUP_DIM), lambda b, g: (0, 0)),
            pl.BlockSpec((None, None, FNET_GROUP_DIM, FNET_GROUP_DIM), lambda b, g: (layer, g, 0, 0)),
        ],
        out_specs=pl.BlockSpec((None, seq, FNET_GROUP_DIM), lambda b, g: (b, 0, g)),
        out_shape=jax.ShapeDtypeStruct((bsz, seq, FNET_WIDTH), F32),
        compiler_params=_cparams(("parallel", "parallel")),
        name="fnet",
    )(u4, u4, u4, u4, cs, tw, ch, w_fnet_bf16)
    return out.reshape(bsz * seq, FNET_WIDTH)


PAIR_LEVELS = ((CHUNK, 16), (16, 4))
PAIR_PARTS = 3
BAND = 4
PREPARE_GROUP = 16
OUTPUT_GROUP = 8
LOG2_E = 1.4426950408889634


@functools.lru_cache(maxsize=None)
def _hgrn_tables():
    t = np.arange(CHUNK)
    tri = np.stack([(t[None, :] <= t[:, None]), (t[None, :] >= t[:, None])]).astype(np.float32)
    mask = np.zeros((2, len(PAIR_LEVELS), CHUNK, PAIR_PARTS * CHUNK), np.float32)
    for lvl, (outer, inner) in enumerate(PAIR_LEVELS):
        same = (t[:, None] // outer) == (t[None, :] // outer)
        t_in = (t[:, None] % outer) // inner
        s_in = t[None, :] % outer
        for j in range(PAIR_PARTS):
            cols = slice(j * CHUNK, (j + 1) * CHUNK)
            mask[0, lvl, :, cols] = same & (t_in == j + 1) & (s_in < inner * (j + 1))
            mask[1, lvl, :, cols] = same & (t_in == j) & (s_in >= inner * (j + 1))
    return jnp.asarray(tri, dtype=BF16), jnp.asarray(mask)


def _rows_block(p, rows, block):
    def rep(r, n):
        if 0 <= r < CHUNK:
            return jnp.broadcast_to(p[r:r + 1], (n, HGRN_DK))
        return jnp.zeros((n, HGRN_DK), F32)

    if block >= V7X_SUBLANES:
        return jnp.concatenate([rep(r, block) for r in rows], axis=0)
    upper = lax.broadcasted_iota(jnp.int32, (V7X_SUBLANES, HGRN_DK), 0) < block
    return jnp.concatenate([jnp.where(upper, rep(rows[2 * i], V7X_SUBLANES), rep(rows[2 * i + 1], V7X_SUBLANES))
                            for i in range(len(rows) // 2)], axis=0)


def _level_operands(q, k, p, reverse, outer, inner):
    q_rows = [inner * (ib + 1) if reverse else inner * ib - 1 for ib in range(CHUNK // inner)]
    q_l = (q * jnp.exp2(p - _rows_block(p, q_rows, inner))).astype(BF16)
    parts = []
    for j in range(PAIR_PARTS):
        k_rows = [outer * ob + inner * (j + 1) - (0 if reverse else 1) for ob in range(CHUNK // outer)]
        parts.append((k * jnp.exp2(jnp.minimum(_rows_block(p, k_rows, outer) - p, 0.0))).astype(BF16))
    return q_l, jnp.concatenate(parts, axis=0)


def _chunk_rows(c):
    return pl.ds(pl.multiple_of(c * CHUNK, CHUNK), CHUNK)


def _hgrn_prepare(cs, reverse, k_ref, g_ref, v_ref, tri_ref, p_ref, ut_ref, de_ref):
    rows = [_chunk_rows(c) for c in cs]
    g_all = jnp.concatenate([g_ref[r, :] for r in rows], axis=1)
    tri = tri_ref[1 if reverse else 0]
    g_1 = g_all.astype(BF16)
    g_2 = (g_all - g_1.astype(F32)).astype(BF16)
    g_3 = (g_all - g_1.astype(F32) - g_2.astype(F32)).astype(BF16)
    p_all = LOG2_E * (jnp.dot(tri, g_1, preferred_element_type=F32) + jnp.dot(tri, g_2, preferred_element_type=F32)
                      + jnp.dot(tri, g_3, preferred_element_type=F32))
    ps = [p_all[:, i * HGRN_DK:(i + 1) * HGRN_DK] for i in range(len(cs))]
    edges = [p[0:1] if reverse else p[CHUNK - 1:CHUNK] for p in ps]
    kdecs = [(k_ref[r, :] * jnp.exp2(e - p)).astype(BF16) for r, e, p in zip(rows, edges, ps)]
    uts = [lax.dot_general(v_ref[r, :].astype(BF16), kd, (((0,), (0,)), ((), ())), preferred_element_type=F32)
           for r, kd in zip(rows, kdecs)]
    for c, r, p, e, ut in zip(cs, rows, ps, edges, uts):
        p_ref[r, :] = p
        ut_ref[c] = ut
        de_ref[c] = jnp.broadcast_to(jnp.exp2(e), (V7X_SUBLANES, HGRN_DK))


def _hgrn_scan(i, state, reverse, n_chunks, st_ref, ut_ref, de_ref):
    c = n_chunks - 1 - i if reverse else i
    st_ref[c] = state.astype(BF16)
    return state * de_ref[c][0:1] + ut_ref[c]


def _hgrn_output(cs, reverse, q_ref, k_ref, v_ref, p_ref, st_ref, mask_ref, o_ref):
    d = 1 if reverse else 0
    n = len(cs)
    nt = (((1,), (1,)), ((), ()))
    rows = [_chunk_rows(c) for c in cs]
    qs = [q_ref[r, :] for r in rows]
    ks = [k_ref[r, :] for r in rows]
    vs = [v_ref[r, :] for r in rows]
    ps = [p_ref[r, :] for r in rows]
    v_parts = [jnp.concatenate([v.astype(BF16)] * PAIR_PARTS, axis=0) for v in vs]
    outs = [lax.dot_general((qs[i] * jnp.exp2(ps[i])).astype(BF16), st_ref[cs[i]], nt, preferred_element_type=F32)
            for i in range(n)]
    for lvl, (outer, inner) in enumerate(PAIR_LEVELS):
        operands = [_level_operands(qs[i], ks[i], ps[i], reverse, outer, inner) for i in range(n)]
        scores = [lax.dot_general(q_l, k_l, nt, preferred_element_type=F32) for q_l, k_l in operands]
        weights = [(s * mask_ref[d, lvl]).astype(BF16) for s in scores]
        outs = [outs[i] + jnp.dot(weights[i], v_parts[i], preferred_element_type=F32) for i in range(n)]

    t_in = lax.broadcasted_iota(jnp.int32, (CHUNK, 1), 0) % BAND
    for i in range(n):
        q, k, v, p = qs[i], ks[i], vs[i], ps[i]
        out = outs[i] + jnp.sum(q * k, axis=-1, keepdims=True) * v
        for delta in range(1, BAND):
            shift = CHUNK - delta if reverse else delta
            k_s = pltpu.roll(k, shift, 0)
            p_s = pltpu.roll(p, shift, 0)
            v_s = pltpu.roll(v, shift, 0)
            valid = (t_in <= BAND - 1 - delta) if reverse else (t_in >= delta)
            a = jnp.sum(q * k_s * jnp.exp2(jnp.minimum(p - p_s, 0.0)), axis=-1, keepdims=True)
            out = out + jnp.where(valid, a, 0.0) * v_s
        if reverse:
            o_ref[rows[i], :] = o_ref[rows[i], :] + out
        else:
            o_ref[rows[i], :] = out


def _hgrn_kernel(q_ref, kf_ref, kb_ref, lf_ref, lb_ref, v_ref, gate_ref, nw_ref, tri_ref, mask_ref,
                 y_ref, o_ref, p_ref, ut_ref, st_ref, de_ref):
    seq = q_ref.shape[0]
    n_chunks = seq // CHUNK

    def per_chunk(fn, group):
        def body(i, carry):
            fn([i * group + u for u in range(group)])
            return carry
        lax.fori_loop(0, n_chunks // group, body, 0)

    for reverse, k_ref, g_ref in ((False, kf_ref, lf_ref), (True, kb_ref, lb_ref)):
        per_chunk(functools.partial(_hgrn_prepare, reverse=reverse, k_ref=k_ref, g_ref=g_ref, v_ref=v_ref,
                                    tri_ref=tri_ref, p_ref=p_ref, ut_ref=ut_ref, de_ref=de_ref), PREPARE_GROUP)
        lax.fori_loop(0, n_chunks,
                      functools.partial(_hgrn_scan, reverse=reverse, n_chunks=n_chunks, st_ref=st_ref,
                                        ut_ref=ut_ref, de_ref=de_ref),
                      jnp.zeros((HGRN_DV, HGRN_DK), F32))
        per_chunk(functools.partial(_hgrn_output, reverse=reverse, q_ref=q_ref, k_ref=k_ref, v_ref=v_ref,
                                    p_ref=p_ref, st_ref=st_ref, mask_ref=mask_ref, o_ref=o_ref), OUTPUT_GROUP)
    o = o_ref[...]
    ms = jnp.mean(o * o, axis=-1, keepdims=True)
    y_ref[...] = o * lax.rsqrt(ms + RMS_EPS) * nw_ref[...] * gate_ref[...]


def _hgrn(q, kf, kb, lf, lb, v, gate, out_norm, layer, bsz, seq):
    tri, mask = _hgrn_tables()
    n_chunks = seq // CHUNK
    as3 = lambda a: a.reshape(bsz, seq, QK_WIDTH)
    head = pl.BlockSpec((None, seq, HGRN_DK), lambda b, h: (b, 0, h))
    out = pl.pallas_call(
        _hgrn_kernel,
        grid=(bsz, HGRN_HEADS),
        in_specs=[head] * 7 + [
            pl.BlockSpec((None, 1, HGRN_DV), lambda b, h: (layer, 0, 0)),
            pl.BlockSpec(tri.shape, lambda b, h: (0, 0, 0)),
            pl.BlockSpec(mask.shape, lambda b, h: (0, 0, 0, 0)),
        ],
        out_specs=head,
        out_shape=jax.ShapeDtypeStruct((bsz, seq, V_WIDTH), F32),
        scratch_shapes=[
            pltpu.VMEM((seq, HGRN_DV), F32),
            pltpu.VMEM((seq, HGRN_DK), F32),
            pltpu.VMEM((n_chunks, HGRN_DV, HGRN_DK), F32),
            pltpu.VMEM((n_chunks, HGRN_DV, HGRN_DK), BF16),
            pltpu.VMEM((n_chunks, V7X_SUBLANES, HGRN_DK), F32),
        ],
        compiler_params=_cparams(("parallel", "parallel")),
        name="hgrn",
    )(as3(q), as3(kf), as3(kb), as3(lf), as3(lb), as3(v), as3(gate), out_norm, tri, mask)
    return out.reshape(bsz * seq, V_WIDTH)


def _outproj_kernel(yf_ref, yr_ref, x_ref, w_ref, nw_ref, wr_ref, x2_ref, h2_ref, aff_ref):
    x2 = (x_ref[...]
          + jnp.dot(yf_ref[...].astype(BF16), w_ref[:FNET_WIDTH, :], preferred_element_type=F32)
          + jnp.dot(yr_ref[...].astype(BF16), w_ref[FNET_WIDTH:, :], preferred_element_type=F32))
    x2_ref[...] = x2
    ms = jnp.mean(x2 * x2, axis=-1, keepdims=True)
    h2 = x2 * lax.rsqrt(ms + RMS_EPS) * nw_ref[...]
    h2_ref[...] = h2
    nt = (((1,), (1,)), ((), ()))
    h_hi = h2.astype(BF16)
    h_lo = (h2 - h_hi.astype(F32)).astype(BF16)
    w = wr_ref[...]
    w_hi = w.astype(BF16)
    w_lo = (w - w_hi.astype(F32)).astype(BF16)
    both = lax.dot_general(jnp.concatenate([w_hi, w_lo], axis=0), h_hi, nt, preferred_element_type=F32)
    logits = (both[:N_EXPERTS] + both[N_EXPERTS:]
              + lax.dot_general(w_hi, h_lo, nt, preferred_element_type=F32))
    e = jnp.exp(logits - jnp.max(logits, axis=0, keepdims=True))
    aff = e / jnp.sum(e, axis=0, keepdims=True)
    for j in range(aff_ref.shape[0]):
        aff_ref[j] = aff[:, j * ROUTE_CHUNK:(j + 1) * ROUTE_CHUNK]


def _outproj(yf, yr, x, w_out_bf16, norm_w, w_router_t, layer):
    n_tok = x.shape[0]
    tm = min(ROW_TILE, n_tok)
    row = lambda i: (i, 0)
    lay3 = lambda i: (layer, 0, 0)
    return pl.pallas_call(
        _outproj_kernel,
        grid=(n_tok // tm,),
        in_specs=[
            pl.BlockSpec((tm, FNET_WIDTH), row),
            pl.BlockSpec((tm, V_WIDTH), row),
            pl.BlockSpec((tm, D_MODEL), row),
            pl.BlockSpec((None, D_MODEL, D_MODEL), lay3),
            pl.BlockSpec((None, 1, D_MODEL), lay3),
            pl.BlockSpec((None, N_EXPERTS, D_MODEL), lay3),
        ],
        out_specs=[
            pl.BlockSpec((tm, D_MODEL), row),
            pl.BlockSpec((tm, D_MODEL), row),
            pl.BlockSpec((tm // ROUTE_CHUNK, N_EXPERTS, ROUTE_CHUNK), lambda i: (i, 0, 0)),
        ],
        out_shape=[
            jax.ShapeDtypeStruct((n_tok, D_MODEL), F32),
            jax.ShapeDtypeStruct((n_tok, D_MODEL), F32),
            jax.ShapeDtypeStruct((n_tok // ROUTE_CHUNK, N_EXPERTS, ROUTE_CHUNK), F32),
        ],
        compiler_params=_cparams(("parallel",)),
        name="outproj",
    )(yf, yr, x, w_out_bf16, norm_w, w_router_t)


ONE_F32_BITS = 0x3F800000
BISECT_STEPS = 31


def _select_kernel(aff_ref, tri_ref, slot_ref, off_ref, *, cap):
    n_chunks = aff_ref.shape[0]
    bits = pltpu.bitcast(aff_ref[...], jnp.int32)

    def count(pred):
        return jnp.sum(jnp.sum(pred.astype(F32), axis=0), axis=1, keepdims=True)

    def bisect(_, lo_hi):
        lo, hi = lo_hi
        mid = lo + ((hi - lo + 1) >> 1)
        ok = count(bits >= mid[None]) >= cap
        return jnp.where(ok, mid, lo), jnp.where(ok, hi, mid - 1)

    cut, _ = lax.fori_loop(0, BISECT_STEPS, bisect,
                           (jnp.zeros((N_EXPERTS, 1), jnp.int32),
                            jnp.full((N_EXPERTS, 1), ONE_F32_BITS, jnp.int32)))
    ties_wanted = cap - count(bits > cut[None])

    def scan(c, carry):
        n_sel, n_tie = carry
        b = pltpu.bitcast(aff_ref[c], jnp.int32)
        above = b > cut
        tie = (b == cut).astype(F32)
        tie_rank = n_tie + jnp.dot(tie.astype(BF16), tri_ref[...], preferred_element_type=F32) - tie
        sel = jnp.where(above, 1.0, jnp.where(tie_rank < ties_wanted, tie, 0.0))
        slot = n_sel + jnp.dot(sel.astype(BF16), tri_ref[...], preferred_element_type=F32) - sel
        slot_ref[c] = jnp.where(sel > 0.0, slot, -1.0).astype(jnp.int32)
        off_ref[c] = jnp.broadcast_to(n_sel, (N_EXPERTS, ROUTE_CHUNK)).astype(jnp.int32)
        return (n_sel + jnp.sum(sel, axis=1, keepdims=True), n_tie + jnp.sum(tie, axis=1, keepdims=True))

    zero = jnp.zeros((N_EXPERTS, 1), F32)
    lax.fori_loop(0, n_chunks, scan, (zero, zero))


@functools.lru_cache(maxsize=None)
def _prefix_table():
    t = np.arange(ROUTE_CHUNK)
    return jnp.asarray((t[:, None] <= t[None, :]).astype(np.float32), dtype=BF16)


def _select(aff, cap):
    n_chunks = aff.shape[0]
    sds = jax.ShapeDtypeStruct((n_chunks, N_EXPERTS, ROUTE_CHUNK), jnp.int32)
    return pl.pallas_call(
        functools.partial(_select_kernel, cap=cap),
        out_shape=[sds, sds],
        compiler_params=pltpu.CompilerParams(vmem_limit_bytes=V7X_VMEM_LIMIT_BYTES),
        name="route_select",
    )(aff, _prefix_table())


PLACE_ROWS = 8


def _place_kernel(off_sm, slot_ref, aff_ref, idx_ref, gate_ref, acc_ref):
    n_chunks = slot_ref.shape[0]
    n_tiles = idx_ref.shape[1]
    acc_ref[...] = jnp.zeros_like(acc_ref)
    lane = lax.broadcasted_iota(jnp.int32, (1, ROUTE_CHUNK), 1)
    j_iota = lax.broadcasted_iota(jnp.int32, (ROUTE_CHUNK, ROUTE_CHUNK), 0)
    lane8 = lax.broadcasted_iota(jnp.int32, (PLACE_ROWS, ROUTE_CHUNK), 1)
    lo_row = lane.astype(F32)

    def chunk(c, carry):
        slots = slot_ref[c]
        aff = aff_ref[c]
        hi_row = jnp.full((1, ROUTE_CHUNK), c, jnp.int32).astype(F32)
        for e in range(N_EXPERTS):
            off = off_sm[c * N_EXPERTS + e]
            tile = off >> 7
            shift = off & (ROUTE_CHUNK - 1)
            local = slots[e:e + 1] - off
            onehot = jnp.where(local == j_iota, 1.0, 0.0).astype(BF16)
            a = aff[e:e + 1]
            g1 = a.astype(BF16)
            g2 = (a - g1.astype(F32)).astype(BF16)
            g3 = (a - g1.astype(F32) - g2.astype(F32)).astype(BF16)
            lhs = jnp.concatenate([hi_row.astype(BF16), lo_row.astype(BF16), g1, g2, g3,
                                   jnp.zeros((PLACE_ROWS - 5, ROUTE_CHUNK), BF16)], axis=0)
            packed = lax.dot_general(lhs, onehot, (((1,), (1,)), ((), ())), preferred_element_type=F32)
            rolled = pltpu.roll(packed, shift, 1)
            acc_ref[e, tile] = acc_ref[e, tile] + jnp.where(lane8 >= shift, rolled, 0.0)
            acc_ref[e, tile + 1] = acc_ref[e, tile + 1] + jnp.where(lane8 < shift, rolled, 0.0)
        return carry

    lax.fori_loop(0, n_chunks, chunk, 0)
    for e in range(N_EXPERTS):
        hi = acc_ref[e, :, 0, :]
        lo = acc_ref[e, :, 1, :]
        idx_ref[e] = (hi * float(ROUTE_CHUNK) + lo).astype(jnp.int32)[:n_tiles]
        gate_ref[e] = (acc_ref[e, :, 2, :] + acc_ref[e, :, 3, :] + acc_ref[e, :, 4, :])[:n_tiles]


def _place(offsets_flat, slots, aff, cap):
    n_tiles = cap // ROUTE_CHUNK
    return pl.pallas_call(
        _place_kernel,
        grid_spec=pltpu.PrefetchScalarGridSpec(
            num_scalar_prefetch=1,
            grid=(1,),
            in_specs=[pl.BlockSpec(slots.shape, lambda i, off: (0, 0, 0)),
                      pl.BlockSpec(aff.shape, lambda i, off: (0, 0, 0))],
            out_specs=[pl.BlockSpec((N_EXPERTS, n_tiles, ROUTE_CHUNK), lambda i, off: (0, 0, 0))] * 2,
            scratch_shapes=[pltpu.VMEM((N_EXPERTS, n_tiles + 2, PLACE_ROWS, ROUTE_CHUNK), F32)],
        ),
        out_shape=[jax.ShapeDtypeStruct((N_EXPERTS, n_tiles, ROUTE_CHUNK), jnp.int32),
                   jax.ShapeDtypeStruct((N_EXPERTS, n_tiles, ROUTE_CHUNK), F32)],
        compiler_params=_cparams(("arbitrary",)),
        name="route_place",
    )(offsets_flat, slots, aff)


FFN_CHUNKS = 8
FFN_BUFFERS = 3


def _ffn_kernel(idx_sm, h_hbm, gate_ref, wg_ref, wu_ref, wd_ref, y_ref, xbuf, sem):
    n_k = pl.num_programs(1)
    step = pl.program_id(0) * n_k + pl.program_id(1)
    n_steps = pl.num_programs(0) * n_k
    slot = step % FFN_BUFFERS

    def row_copy(tile, buf, j):
        row = idx_sm[tile * SLOT_TILE + j]
        return pltpu.make_async_copy(h_hbm.at[pl.ds(row, 1)], xbuf.at[buf, pl.ds(j, 1)], sem.at[buf])

    def gather(tile, buf):
        def body(j, carry):
            row_copy(tile, buf, j).start()
            return carry
        lax.fori_loop(0, SLOT_TILE, body, 0)

    @pl.when(step == 0)
    def _():
        gather(0, 0)

    @pl.when((step == 0) & (n_steps > 1))
    def _():
        gather(1, 1)

    pltpu.make_async_copy(h_hbm.at[pl.ds(0, SLOT_TILE)], xbuf.at[slot], sem.at[slot]).wait()
    x = xbuf[slot].astype(BF16)

    width = D_FF_EXPERT // FFN_CHUNKS
    acc = None
    for f in range(FFN_CHUNKS):
        cols = slice(f * width, (f + 1) * width)
        hg = jnp.dot(x, wg_ref[:, cols], preferred_element_type=F32)
        hu = jnp.dot(x, wu_ref[:, cols], preferred_element_type=F32)
        he = (hg * _sigmoid(hg) * hu).astype(BF16)
        part = jnp.dot(he, wd_ref[cols, :], preferred_element_type=F32)
        acc = part if acc is None else acc + part
    y_ref[...] = acc * gate_ref[...]

    @pl.when(step + 2 < n_steps)
    def _():
        for j in range(SLOT_TILE):
            row_copy(step + 2, (step + 2) % FFN_BUFFERS, j).start()


def _ffn(idx_flat, h, gate_col, wg_bf16, wu_bf16, wd_bf16, layer, cap):
    assert cap % SLOT_TILE == 0
    n_k = cap // SLOT_TILE
    wspec = lambda shape: pl.BlockSpec((None, None) + shape, lambda e, k, idx: (layer, e, 0, 0))
    tile = lambda e, k, idx: (e * n_k + k, 0)
    return pl.pallas_call(
        _ffn_kernel,
        grid_spec=pltpu.PrefetchScalarGridSpec(
            num_scalar_prefetch=1,
            grid=(N_EXPERTS, n_k),
            in_specs=[pl.BlockSpec(memory_space=pl.ANY),
                      pl.BlockSpec((SLOT_TILE, 1), tile),
                      wspec((D_MODEL, D_FF_EXPERT)), wspec((D_MODEL, D_FF_EXPERT)),
                      wspec((D_FF_EXPERT, D_MODEL))],
            out_specs=pl.BlockSpec((SLOT_TILE, D_MODEL), tile),
            scratch_shapes=[pltpu.VMEM((FFN_BUFFERS, SLOT_TILE, D_MODEL), F32),
                            pltpu.SemaphoreType.DMA((FFN_BUFFERS,))],
        ),
        out_shape=jax.ShapeDtypeStruct((N_EXPERTS * cap, D_MODEL), F32),
        compiler_params=_cparams(("arbitrary", "arbitrary")),
        name="expert_ffn",
    )(idx_flat, h, gate_col, wg_bf16, wu_bf16, wd_bf16)


PIECE = 32
SLOT_SPLIT_BITS = 6
SLOT_SPLIT = 1 << SLOT_SPLIT_BITS
PIECES_PER_EXPERT = -(-(ROUTE_CHUNK + 2 * (V7X_SUBLANES - 1)) // PIECE)
PACK_ROWS = N_EXPERTS * PIECES_PER_EXPERT * PIECE
PACK_BLOCK = 256


def _combine_kernel(off_sm, x_ref, slot_ref, y_hbm, o_ref, buf, sem, *, cap):
    i = pl.program_id(0)
    n_tiles = pl.num_programs(0)
    cur = i % 2

    def plan(tile):
        starts, pieces, bases = [], [], []
        base = 0
        for e in range(N_EXPERTS):
            s0 = off_sm[tile * N_EXPERTS + e]
            s1 = off_sm[(tile + 1) * N_EXPERTS + e]
            a0 = (s0 >> 3) << 3
            rows = jnp.where(s1 > s0, (((s1 + V7X_SUBLANES - 1) >> 3) << 3) - a0, 0)
            n_p = (rows + PIECE - 1) // PIECE
            starts.append(jnp.minimum(a0, cap - n_p * PIECE))
            pieces.append(n_p)
            bases.append(base)
            base = base + n_p * PIECE
        return starts, pieces, bases, base

    def piece_copy(src_row, b, dst_row):
        return pltpu.make_async_copy(y_hbm.at[pl.ds(pl.multiple_of(src_row, V7X_SUBLANES), PIECE)],
                                     buf.at[b, pl.ds(pl.multiple_of(dst_row, V7X_SUBLANES), PIECE)], sem.at[b])

    def fetch(tile, b):
        starts, pieces, bases, _ = plan(tile)
        for e in range(N_EXPERTS):
            def body(j, carry, e=e):
                piece_copy(e * cap + starts[e] + j * PIECE, b, bases[e] + j * PIECE).start()
                return carry
            lax.fori_loop(0, pieces[e], body, 0)

    @pl.when(i == 0)
    def _():
        buf[...] = jnp.zeros_like(buf)
        fetch(0, 0)

    @pl.when(i + 1 < n_tiles)
    def _():
        fetch(i + 1, 1 - cur)

    starts, pieces, bases, total = plan(i)

    def wait_piece(j, carry):
        piece_copy(0, cur, 0).wait()
        return carry
    lax.fori_loop(0, total // PIECE, wait_piece, 0)

    slot1 = slot_ref[...] + 1
    parts = jnp.concatenate([(slot1 >> SLOT_SPLIT_BITS).astype(F32), (slot1 & (SLOT_SPLIT - 1)).astype(F32),
                             jnp.zeros((ROUTE_CHUNK - 2 * N_EXPERTS, ROUTE_CHUNK), F32)], axis=0)
    slot_parts = parts.T[:, :2 * N_EXPERTS].astype(BF16)
    e_iota = lax.broadcasted_iota(jnp.int32, (N_EXPERTS, 1), 0)
    first = jnp.zeros((N_EXPERTS, 1), F32)
    last = jnp.zeros((N_EXPERTS, 1), F32)
    shift = jnp.zeros((N_EXPERTS, 1), F32)
    as_f32 = lambda v: jnp.asarray(v, jnp.int32).astype(F32)
    for e in range(N_EXPERTS):
        here = e_iota == e
        first = jnp.where(here, as_f32(bases[e]), first)
        last = jnp.where(here, as_f32(bases[e] + pieces[e] * PIECE), last)
        shift = jnp.where(here, as_f32(bases[e] - starts[e]), shift)

    o_ref[...] = x_ref[...]
    lane = lax.broadcasted_iota(jnp.int32, (1, PACK_BLOCK), 1).astype(F32)

    def block(kb, carry):
        rows = pl.ds(pl.multiple_of(kb * PACK_BLOCK, PACK_BLOCK), PACK_BLOCK)
        row_id = lane + (kb * PACK_BLOCK).astype(F32)
        member = jnp.where((row_id >= first) & (row_id < last), 1.0, 0.0)
        spread = jnp.concatenate([member * float(SLOT_SPLIT), member], axis=0).astype(BF16)
        slot_of_row = jnp.dot(slot_parts, spread, preferred_element_type=F32)
        wanted = row_id - jnp.sum(member * shift, axis=0, keepdims=True) + 1.0
        onehot = jnp.where(slot_of_row == wanted, 1.0, 0.0).astype(BF16)
        y = buf[cur, rows, :]
        y_hi = y.astype(BF16)
        y_lo = (y - y_hi.astype(F32)).astype(BF16)
        o_ref[...] = (o_ref[...] + jnp.dot(onehot, y_hi, preferred_element_type=F32)
                      + jnp.dot(onehot, y_lo, preferred_element_type=F32))
        return carry
    lax.fori_loop(0, (total + PACK_BLOCK - 1) // PACK_BLOCK, block, 0)


def _combine(offsets_flat, x, slots, ye, cap):
    assert cap < SLOT_SPLIT * 256 and cap % PIECE == 0
    n_tok = x.shape[0]
    tok = pl.BlockSpec((ROUTE_CHUNK, D_MODEL), lambda i, off: (i, 0))
    return pl.pallas_call(
        functools.partial(_combine_kernel, cap=cap),
        grid_spec=pltpu.PrefetchScalarGridSpec(
            num_scalar_prefetch=1,
            grid=(n_tok // ROUTE_CHUNK,),
            in_specs=[tok,
                      pl.BlockSpec((None, N_EXPERTS, ROUTE_CHUNK), lambda i, off: (i, 0, 0)),
                      pl.BlockSpec(memory_space=pl.ANY)],
            out_specs=tok,
            scratch_shapes=[pltpu.VMEM((2, PACK_ROWS, D_MODEL), F32), pltpu.SemaphoreType.DMA((2,))],
        ),
        out_shape=jax.ShapeDtypeStruct((n_tok, D_MODEL), F32),
        compiler_params=_cparams(("arbitrary",)),
        name="combine",
    )(offsets_flat, x, slots, ye)


def _final_norm_kernel(x_ref, nw_ref, o_ref):
    x = x_ref[...]
    ms = jnp.mean(x * x, axis=-1, keepdims=True)
    o_ref[...] = x * lax.rsqrt(ms + RMS_EPS) * nw_ref[...]


def _final_norm(x, norm_w):
    n_tok = x.shape[0]
    tm = min(ROW_TILE, n_tok)
    return pl.pallas_call(
        _final_norm_kernel,
        grid=(n_tok // tm,),
        in_specs=[pl.BlockSpec((tm, D_MODEL), lambda i: (i, 0)), pl.BlockSpec((1, D_MODEL), lambda i: (0, 0))],
        out_specs=pl.BlockSpec((tm, D_MODEL), lambda i: (i, 0)),
        out_shape=jax.ShapeDtypeStruct((n_tok, D_MODEL), F32),
        compiler_params=_cparams(("parallel",)),
        name="final_norm",
    )(x, norm_w)


def _moe(x2, h2, aff, params, layer):
    n_tok = x2.shape[0]
    cap = EC_CAPACITY_FACTOR * n_tok // N_EXPERTS
    slots, offs = _select(aff, cap)
    offsets_flat = jnp.concatenate(
        [offs[:, :, 0], jnp.full((1, N_EXPERTS), cap, jnp.int32)], axis=0).reshape(-1)
    idx, gate = _place(offsets_flat, slots, aff, cap)
    ye = _ffn(idx.reshape(-1), h2, gate.reshape(-1, 1), params["w_gate"], params["w_up"], params["w_down"],
              layer, cap)
    return _combine(offsets_flat, x2, slots, ye, cap)


def _trunk(x, params):
    bsz, seq, _ = x.shape
    x = x.reshape(bsz * seq, D_MODEL)
    for layer in range(DEPTH):
        u, q, kf, kb, lf, lb, v, g = _inproj(x, params["norm_mix"], params["w_in"], params["lbf"], params["lbb"], layer)
        yf = _fnet(u, params["w_fnet"], layer, bsz, seq)
        yr = _hgrn(q, kf, kb, lf, lb, v, g, params["hgrn_out_norm"], layer, bsz, seq)
        x2, h2, aff = _outproj(yf, yr, x, params["w_out"], params["norm_ffn"], params["w_router_t"], layer)
        x = _moe(x2, h2, aff, params, layer)
    return _final_norm(x, params["norm_final"]).reshape(bsz, seq, D_MODEL)


def kernel(x_prompt, x_sample, norm_mix, w_in, w_fnet, lb_fwd, lb_bwd, hgrn_out_norm, w_out,
           norm_ffn, w_router, w_gate, w_up, w_down, norm_final):
    params = {
        "norm_mix": norm_mix.reshape(DEPTH, 1, D_MODEL),
        "w_in": w_in.astype(BF16),
        "w_fnet": w_fnet.astype(BF16),
        "lbf": _lower_bounds(lb_fwd).reshape(DEPTH, 1, QK_WIDTH),
        "lbb": _lower_bounds(lb_bwd).reshape(DEPTH, 1, QK_WIDTH),
        "hgrn_out_norm": hgrn_out_norm.reshape(DEPTH, 1, HGRN_DV),
        "w_out": w_out.astype(BF16),
        "norm_ffn": norm_ffn.reshape(DEPTH, 1, D_MODEL),
        "w_router_t": jnp.swapaxes(w_router, 1, 2),
        "w_gate": w_gate.astype(BF16),
        "w_up": w_up.astype(BF16),
        "w_down": w_down.astype(BF16),
        "norm_final": norm_final.reshape(1, D_MODEL),
    }
    return _trunk(x_prompt, params), _trunk(x_sample, params)
```

```python
import functools

import numpy as np
import jax
import jax.numpy as jnp
from jax import lax
from jax.experimental import pallas as pl
from jax.experimental.pallas import tpu as pltpu

D_MODEL = 1024
DEPTH = 4
FNET_WIDTH = 512
FNET_GROUPS = 4
FNET_GROUP_DIM = 128
HGRN_HEADS = 4
HGRN_DK = 128
HGRN_DV = 128
QK_WIDTH = 512
V_WIDTH = 512
IN_WIDTH = FNET_WIDTH + 3 * QK_WIDTH + 2 * V_WIDTH
N_EXPERTS = 16
EC_CAPACITY_FACTOR = 2
D_FF_EXPERT = 2048
RMS_EPS = 1e-6

V7X_LANES = 128
V7X_SUBLANES = 8
V7X_VMEM_LIMIT_BYTES = 56 * 1024 * 1024

BF16 = jnp.bfloat16
F32 = jnp.float32

CHUNK = 64

ROW_TILE = 512
SLOT_TILE = 512
ROUTE_CHUNK = 128


def _cparams(sem, vmem=V7X_VMEM_LIMIT_BYTES):
    return pltpu.CompilerParams(dimension_semantics=sem, vmem_limit_bytes=vmem)


def _sigmoid(x):
    return 1.0 / (1.0 + jnp.exp(-x))


def _lower_bounds_kernel(p_ref, o_ref):
    p = p_ref[...]
    m = jnp.max(p, axis=0, keepdims=True)
    e = jnp.exp(p - m)
    sm = e / jnp.sum(e, axis=0, keepdims=True)
    acc = None
    rows = []
    for layer in range(DEPTH):
        acc = sm[layer:layer + 1] if acc is None else acc + sm[layer:layer + 1]
        rows.append(jnp.maximum(acc - sm[0:1], 0.0))
    o_ref[...] = jnp.concatenate(rows, axis=0)


def _lower_bounds(lb_param):
    return pl.pallas_call(
        _lower_bounds_kernel,
        out_shape=jax.ShapeDtypeStruct((DEPTH, QK_WIDTH), F32),
        name="lower_bounds",
    )(lb_param)


def _hgrn_gates(z, lb):
    e = jnp.exp(-jnp.abs(z))
    r = 1.0 / (1.0 + e)
    sig_neg = jnp.where(z >= 0, e * r, r)
    k = (1.0 - lb) * sig_neg
    log_sig = jnp.minimum(z, 0.0) - jnp.log(1.0 + e)
    c = jnp.log1p(-lb) + log_sig
    has_lb = lb > 0.0
    a = jnp.log(jnp.where(has_lb, lb, 1.0))
    logf = jnp.where(has_lb, jnp.maximum(a, c) + jnp.log(1.0 + jnp.exp(-jnp.abs(a - c))), c)
    return k, logf


def _inproj_kernel(x_ref, nw_ref, w_ref, lbf_ref, lbb_ref,
                   u_ref, q_ref, kf_ref, kb_ref, lf_ref, lbo_ref, v_ref, g_ref):
    x = x_ref[...]
    ms = jnp.mean(x * x, axis=-1, keepdims=True)
    h = (x * lax.rsqrt(ms + RMS_EPS) * nw_ref[...]).astype(BF16)

    def proj(col0, width):
        return jnp.dot(h, w_ref[:, col0:col0 + width], preferred_element_type=F32)

    u_ref[...] = proj(0, FNET_WIDTH).astype(u_ref.dtype)
    q = proj(FNET_WIDTH, QK_WIDTH)
    q_ref[...] = (q * _sigmoid(q)).astype(q_ref.dtype)
    kf, lf = _hgrn_gates(proj(FNET_WIDTH + QK_WIDTH, QK_WIDTH), lbf_ref[...])
    kf_ref[...] = kf.astype(kf_ref.dtype)
    lf_ref[...] = lf
    kb, lb = _hgrn_gates(proj(FNET_WIDTH + 2 * QK_WIDTH, QK_WIDTH), lbb_ref[...])
    kb_ref[...] = kb.astype(kb_ref.dtype)
    lbo_ref[...] = lb
    v_ref[...] = proj(FNET_WIDTH + 3 * QK_WIDTH, V_WIDTH).astype(v_ref.dtype)
    g = proj(FNET_WIDTH + 3 * QK_WIDTH + V_WIDTH, V_WIDTH)
    g_ref[...] = (g * _sigmoid(g)).astype(g_ref.dtype)


def _inproj(x, norm_w, w_in_bf16, lbf, lbb, layer):
    n_tok = x.shape[0]
    tm = min(ROW_TILE, n_tok)
    row = lambda i: (i, 0)
    lay3 = lambda i: (layer, 0, 0)
    out_dtypes = (BF16, BF16, BF16, BF16, F32, F32, BF16, BF16)
    return pl.pallas_call(
        _inproj_kernel,
        grid=(n_tok // tm,),
        in_specs=[
            pl.BlockSpec((tm, D_MODEL), row),
            pl.BlockSpec((None, 1, D_MODEL), lay3),
            pl.BlockSpec((None, D_MODEL, IN_WIDTH), lay3),
            pl.BlockSpec((None, 1, QK_WIDTH), lay3),
            pl.BlockSpec((None, 1, QK_WIDTH), lay3),
        ],
        out_specs=[pl.BlockSpec((tm, QK_WIDTH), row)] * 8,
        out_shape=[jax.ShapeDtypeStruct((n_tok, QK_WIDTH), dt) for dt in out_dtypes],
        compiler_params=_cparams(("parallel",)),
        name="inproj",
    )(x, norm_w, w_in_bf16, lbf, lbb)


FNET_RADIX = 4


@functools.lru_cache(maxsize=None)
def _fnet_tables(seq):
    m = seq // FNET_RADIX
    k = np.arange(m, dtype=np.float64)
    ang = 2.0 * np.pi * np.outer(k, k) / m
    cs = np.concatenate([np.cos(ang), np.sin(ang)], axis=0)
    tw = []
    for r in range(1, FNET_RADIX):
        a = 2.0 * np.pi * r * k / seq
        tw.append(np.broadcast_to(np.cos(a)[:, None], (m, V7X_LANES)))
        tw.append(np.broadcast_to(np.sin(a)[:, None], (m, V7X_LANES)))
    tw = np.stack(tw, axis=0)
    c = np.arange(FNET_GROUP_DIM, dtype=np.float64)
    angc = 2.0 * np.pi * np.outer(c, c) / FNET_GROUP_DIM
    scale = 1.0 / np.sqrt(float(seq) * FNET_GROUP_DIM)
    ch = np.concatenate([np.cos(angc), np.sin(angc)], axis=0) * scale
    return (jnp.asarray(cs, dtype=BF16), jnp.asarray(tw, dtype=F32), jnp.asarray(ch, dtype=BF16))


def _fnet_kernel(x0_ref, x1_ref, x2_ref, x3_ref, cs_ref, tw_ref, ch_ref, w_ref, o_ref):
    m = x0_ref.shape[0]
    x = jnp.concatenate([r[...].astype(BF16) for r in (x0_ref, x1_ref, x2_ref, x3_ref)], axis=1)
    pq = jnp.dot(cs_ref[...], x, preferred_element_type=F32)
    h_re, h_im = [], []
    for r in range(FNET_RADIX):
        p = pq[:m, r * V7X_LANES:(r + 1) * V7X_LANES]
        q = pq[m:, r * V7X_LANES:(r + 1) * V7X_LANES]
        if r == 0:
            h_re.append(p)
            h_im.append(-q)
        else:
            tc = tw_ref[2 * (r - 1)]
            ts = tw_ref[2 * (r - 1) + 1]
            h_re.append(tc * p - ts * q)
            h_im.append(-(tc * q + ts * p))
    x_re = [h_re[0] + h_re[1] + h_re[2] + h_re[3],
            h_re[0] + h_im[1] - h_re[2] - h_im[3],
            h_re[0] - h_re[1] + h_re[2] - h_re[3],
            h_re[0] - h_im[1] - h_re[2] + h_im[3]]
    x_im = [h_im[0] + h_im[1] + h_im[2] + h_im[3],
            h_im[0] - h_re[1] - h_im[2] + h_re[3],
            h_im[0] - h_im[1] + h_im[2] - h_im[3],
            h_im[0] + h_re[1] - h_im[2] - h_re[3]]
    xc = jnp.concatenate([jnp.concatenate([a, b], axis=1) for a, b in zip(x_re, x_im)], axis=0)
    mixed = jnp.dot(xc.astype(BF16), ch_ref[...], preferred_element_type=F32)
    o_ref[...] = jnp.dot(mixed.astype(BF16), w_ref[...], preferred_element_type=F32)


def _fnet(u, w_fnet_bf16, layer, bsz, seq):
    m = seq // FNET_RADIX
    cs, tw, ch = _fnet_tables(seq)
    u4 = u.reshape(bsz, m, FNET_RADIX * FNET_WIDTH)
    x_specs = [pl.BlockSpec((None, m, FNET_GROUP_DIM),
                            functools.partial(lambda b, g, r: (b, 0, r * FNET_GROUPS + g), r=r))
               for r in range(FNET_RADIX)]
    out = pl.pallas_call(
        _fnet_kernel,
        grid=(bsz, FNET_GROUPS),
        in_specs=x_specs + [
            pl.BlockSpec((2 * m, m), lambda b, g: (0, 0)),
            pl.BlockSpec((2 * (FNET_RADIX - 1), m, V7X_LANES), lambda b, g: (0, 0, 0)),
            pl.BlockSpec((2 * FNET_GROUP_DIM, FNET_GROUP_DIM), lambda b, g: (0, 0)),
            pl.BlockSpec((None, None, FNET_GROUP_DIM, FNET_GROUP_DIM), lambda b, g: (layer, g, 0, 0)),
        ],
        out_specs=pl.BlockSpec((None, seq, FNET_GROUP_DIM), lambda b, g: (b, 0, g)),
        out_shape=jax.ShapeDtypeStruct((bsz, seq, FNET_WIDTH), F32),
        compiler_params=_cparams(("parallel", "parallel")),
        name="fnet",
    )(u4, u4, u4, u4, cs, tw, ch, w_fnet_bf16)
    return out.reshape(bsz * seq, FNET_WIDTH)


PAIR_LEVELS = ((CHUNK, 16), (16, 4))
PAIR_PARTS = 3
BAND = 4
PREPARE_GROUP = 16
OUTPUT_GROUP = 8
LOG2_E = 1.4426950408889634


@functools.lru_cache(maxsize=None)
def _hgrn_tables():
    t = np.arange(CHUNK)
    tri = np.stack([(t[None, :] <= t[:, None]), (t[None, :] >= t[:, None])]).astype(np.float32)
    mask = np.zeros((2, len(PAIR_LEVELS), CHUNK, PAIR_PARTS * CHUNK), np.float32)
    for lvl, (outer, inner) in enumerate(PAIR_LEVELS):
        same = (t[:, None] // outer) == (t[None, :] // outer)
        t_in = (t[:, None] % outer) // inner
        s_in = t[None, :] % outer
        for j in range(PAIR_PARTS):
            cols = slice(j * CHUNK, (j + 1) * CHUNK)
            mask[0, lvl, :, cols] = same & (t_in == j + 1) & (s_in < inner * (j + 1))
            mask[1, lvl, :, cols] = same & (t_in == j) & (s_in >= inner * (j + 1))
    return jnp.asarray(tri, dtype=BF16), jnp.asarray(mask)


def _rows_block(p, rows, block):
    def rep(r, n):
        if 0 <= r < CHUNK:
            return jnp.broadcast_to(p[r:r + 1], (n, HGRN_DK))
        return jnp.zeros((n, HGRN_DK), F32)

    if block >= V7X_SUBLANES:
        return jnp.concatenate([rep(r, block) for r in rows], axis=0)
    upper = lax.broadcasted_iota(jnp.int32, (V7X_SUBLANES, HGRN_DK), 0) < block
    return jnp.concatenate([jnp.where(upper, rep(rows[2 * i], V7X_SUBLANES), rep(rows[2 * i + 1], V7X_SUBLANES))
                            for i in range(len(rows) // 2)], axis=0)


def _level_operands(q, k, p, reverse, outer, inner):
    q_rows = [inner * (ib + 1) if reverse else inner * ib - 1 for ib in range(CHUNK // inner)]
    q_l = (q * jnp.exp2(p - _rows_block(p, q_rows, inner))).astype(BF16)
    parts = []
    for j in range(PAIR_PARTS):
        k_rows = [outer * ob + inner * (j + 1) - (0 if reverse else 1) for ob in range(CHUNK // outer)]
        parts.append((k * jnp.exp2(jnp.minimum(_rows_block(p, k_rows, outer) - p, 0.0))).astype(BF16))
    return q_l, jnp.concatenate(parts, axis=0)


def _chunk_rows(c):
    return pl.ds(pl.multiple_of(c * CHUNK, CHUNK), CHUNK)


def _hgrn_prepare(cs, reverse, k_ref, g_ref, v_ref, tri_ref, p_ref, ut_ref, de_ref):
    rows = [_chunk_rows(c) for c in cs]
    g_all = jnp.concatenate([g_ref[r, :] for r in rows], axis=1)
    tri = tri_ref[1 if reverse else 0]
    g_1 = g_all.astype(BF16)
    g_2 = (g_all - g_1.astype(F32)).astype(BF16)
    g_3 = (g_all - g_1.astype(F32) - g_2.astype(F32)).astype(BF16)
    p_all = LOG2_E * (jnp.dot(tri, g_1, preferred_element_type=F32) + jnp.dot(tri, g_2, preferred_element_type=F32)
                      + jnp.dot(tri, g_3, preferred_element_type=F32))
    ps = [p_all[:, i * HGRN_DK:(i + 1) * HGRN_DK] for i in range(len(cs))]
    edges = [p[0:1] if reverse else p[CHUNK - 1:CHUNK] for p in ps]
    kdecs = [(k_ref[r, :].astype(F32) * jnp.exp2(e - p)).astype(BF16) for r, e, p in zip(rows, edges, ps)]
    uts = [lax.dot_general(v_ref[r, :].astype(BF16), kd, (((0,), (0,)), ((), ())), preferred_element_type=F32)
           for r, kd in zip(rows, kdecs)]
    for c, r, p, e, ut in zip(cs, rows, ps, edges, uts):
        p_ref[r, :] = p
        ut_ref[c] = ut
        de_ref[c] = jnp.broadcast_to(jnp.exp2(e), (V7X_SUBLANES, HGRN_DK))


def _hgrn_scan(i, state, reverse, n_chunks, st_ref, ut_ref, de_ref):
    c = n_chunks - 1 - i if reverse else i
    st_ref[c] = state.astype(BF16)
    return state * de_ref[c][0:1] + ut_ref[c]


def _hgrn_output(cs, reverse, q_ref, k_ref, v_ref, p_ref, st_ref, mask_ref, o_ref):
    d = 1 if reverse else 0
    n = len(cs)
    nt = (((1,), (1,)), ((), ()))
    rows = [_chunk_rows(c) for c in cs]
    qs = [q_ref[r, :].astype(F32) for r in rows]
    ks = [k_ref[r, :].astype(F32) for r in rows]
    vs = [v_ref[r, :].astype(F32) for r in rows]
    ps = [p_ref[r, :] for r in rows]
    v_parts = [jnp.concatenate([v.astype(BF16)] * PAIR_PARTS, axis=0) for v in vs]
    outs = [lax.dot_general((qs[i] * jnp.exp2(ps[i])).astype(BF16), st_ref[cs[i]], nt, preferred_element_type=F32)
            for i in range(n)]
    for lvl, (outer, inner) in enumerate(PAIR_LEVELS):
        operands = [_level_operands(qs[i], ks[i], ps[i], reverse, outer, inner) for i in range(n)]
        scores = [lax.dot_general(q_l, k_l, nt, preferred_element_type=F32) for q_l, k_l in operands]
        weights = [(s * mask_ref[d, lvl]).astype(BF16) for s in scores]
        outs = [outs[i] + jnp.dot(weights[i], v_parts[i], preferred_element_type=F32) for i in range(n)]

    t_in = lax.broadcasted_iota(jnp.int32, (CHUNK, 1), 0) % BAND
    for i in range(n):
        q, k, v, p = qs[i], ks[i], vs[i], ps[i]
        out = outs[i] + jnp.sum(q * k, axis=-1, keepdims=True) * v
        for delta in range(1, BAND):
            shift = CHUNK - delta if reverse else delta
            k_s = pltpu.roll(k, shift, 0)
            p_s = pltpu.roll(p, shift, 0)
            v_s = pltpu.roll(v, shift, 0)
            valid = (t_in <= BAND - 1 - delta) if reverse else (t_in >= delta)
            a = jnp.sum(q * k_s * jnp.exp2(jnp.minimum(p - p_s, 0.0)), axis=-1, keepdims=True)
            out = out + jnp.where(valid, a, 0.0) * v_s
        if reverse:
            o_ref[rows[i], :] = o_ref[rows[i], :] + out
        else:
            o_ref[rows[i], :] = out


def _hgrn_kernel(q_ref, kf_ref, kb_ref, lf_ref, lb_ref, v_ref, gate_ref, nw_ref, tri_ref, mask_ref,
                 y_ref, o_ref, p_ref, ut_ref, st_ref, de_ref):
    seq = q_ref.shape[0]
    n_chunks = seq // CHUNK

    def per_chunk(fn, group):
        def body(i, carry):
            fn([i * group + u for u in range(group)])
            return carry
        lax.fori_loop(0, n_chunks // group, body, 0)

    for reverse, k_ref, g_ref in ((False, kf_ref, lf_ref), (True, kb_ref, lb_ref)):
        per_chunk(functools.partial(_hgrn_prepare, reverse=reverse, k_ref=k_ref, g_ref=g_ref, v_ref=v_ref,
                                    tri_ref=tri_ref, p_ref=p_ref, ut_ref=ut_ref, de_ref=de_ref), PREPARE_GROUP)
        lax.fori_loop(0, n_chunks,
                      functools.partial(_hgrn_scan, reverse=reverse, n_chunks=n_chunks, st_ref=st_ref,
                                        ut_ref=ut_ref, de_ref=de_ref),
                      jnp.zeros((HGRN_DV, HGRN_DK), F32))
        per_chunk(functools.partial(_hgrn_output, reverse=reverse, q_ref=q_ref, k_ref=k_ref, v_ref=v_ref,
                                    p_ref=p_ref, st_ref=st_ref, mask_ref=mask_ref, o_ref=o_ref), OUTPUT_GROUP)
    o = o_ref[...]
    ms = jnp.mean(o * o, axis=-1, keepdims=True)
    y_ref[...] = o * lax.rsqrt(ms + RMS_EPS) * nw_ref[...] * gate_ref[...].astype(F32)


def _hgrn(q, kf, kb, lf, lb, v, gate, out_norm, layer, bsz, seq):
    tri, mask = _hgrn_tables()
    n_chunks = seq // CHUNK
    as3 = lambda a: a.reshape(bsz, seq, QK_WIDTH)
    head = pl.BlockSpec((None, seq, HGRN_DK), lambda b, h: (b, 0, h))
    out = pl.pallas_call(
        _hgrn_kernel,
        grid=(bsz, HGRN_HEADS),
        in_specs=[head] * 7 + [
            pl.BlockSpec((None, 1, HGRN_DV), lambda b, h: (layer, 0, 0)),
            pl.BlockSpec(tri.shape, lambda b, h: (0, 0, 0)),
            pl.BlockSpec(mask.shape, lambda b, h: (0, 0, 0, 0)),
        ],
        out_specs=head,
        out_shape=jax.ShapeDtypeStruct((bsz, seq, V_WIDTH), F32),
        scratch_shapes=[
            pltpu.VMEM((seq, HGRN_DV), F32),
            pltpu.VMEM((seq, HGRN_DK), F32),
            pltpu.VMEM((n_chunks, HGRN_DV, HGRN_DK), F32),
            pltpu.VMEM((n_chunks, HGRN_DV, HGRN_DK), BF16),
            pltpu.VMEM((n_chunks, V7X_SUBLANES, HGRN_DK), F32),
        ],
        compiler_params=_cparams(("parallel", "parallel")),
        name="hgrn",
    )(as3(q), as3(kf), as3(kb), as3(lf), as3(lb), as3(v), as3(gate), out_norm, tri, mask)
    return out.reshape(bsz * seq, V_WIDTH)


def _outproj_kernel(yf_ref, yr_ref, x_ref, w_ref, nw_ref, wr_ref, x2_ref, h2_ref, aff_ref):
    x2 = (x_ref[...]
          + jnp.dot(yf_ref[...].astype(BF16), w_ref[:FNET_WIDTH, :], preferred_element_type=F32)
          + jnp.dot(yr_ref[...].astype(BF16), w_ref[FNET_WIDTH:, :], preferred_element_type=F32))
    x2_ref[...] = x2
    ms = jnp.mean(x2 * x2, axis=-1, keepdims=True)
    h2 = x2 * lax.rsqrt(ms + RMS_EPS) * nw_ref[...]
    h2_ref[...] = h2
    nt = (((1,), (1,)), ((), ()))
    h_hi = h2.astype(BF16)
    h_lo = (h2 - h_hi.astype(F32)).astype(BF16)
    w = wr_ref[...]
    w_hi = w.astype(BF16)
    w_lo = (w - w_hi.astype(F32)).astype(BF16)
    both = lax.dot_general(jnp.concatenate([w_hi, w_lo], axis=0), h_hi, nt, preferred_element_type=F32)
    logits = (both[:N_EXPERTS] + both[N_EXPERTS:]
              + lax.dot_general(w_hi, h_lo, nt, preferred_element_type=F32))
    e = jnp.exp(logits - jnp.max(logits, axis=0, keepdims=True))
    aff = e / jnp.sum(e, axis=0, keepdims=True)
    for j in range(aff_ref.shape[0]):
        aff_ref[j] = aff[:, j * ROUTE_CHUNK:(j + 1) * ROUTE_CHUNK]


def _outproj(yf, yr, x, w_out_bf16, norm_w, w_router_t, layer):
    n_tok = x.shape[0]
    tm = min(ROW_TILE, n_tok)
    row = lambda i: (i, 0)
    lay3 = lambda i: (layer, 0, 0)
    return pl.pallas_call(
        _outproj_kernel,
        grid=(n_tok // tm,),
        in_specs=[
            pl.BlockSpec((tm, FNET_WIDTH), row),
            pl.BlockSpec((tm, V_WIDTH), row),
            pl.BlockSpec((tm, D_MODEL), row),
            pl.BlockSpec((None, D_MODEL, D_MODEL), lay3),
            pl.BlockSpec((None, 1, D_MODEL), lay3),
            pl.BlockSpec((None, N_EXPERTS, D_MODEL), lay3),
        ],
        out_specs=[
            pl.BlockSpec((tm, D_MODEL), row),
            pl.BlockSpec((tm, D_MODEL), row),
            pl.BlockSpec((tm // ROUTE_CHUNK, N_EXPERTS, ROUTE_CHUNK), lambda i: (i, 0, 0)),
        ],
        out_shape=[
            jax.ShapeDtypeStruct((n_tok, D_MODEL), F32),
            jax.ShapeDtypeStruct((n_tok, D_MODEL), F32),
            jax.ShapeDtypeStruct((n_tok // ROUTE_CHUNK, N_EXPERTS, ROUTE_CHUNK), F32),
        ],
        compiler_params=_cparams(("parallel",)),
        name="outproj",
    )(yf, yr, x, w_out_bf16, norm_w, w_router_t)


ONE_F32_BITS = 0x3F800000
BISECT_STEPS = 31


def _select_kernel(aff_ref, tri_ref, slot_ref, off_ref, *, cap):
    n_chunks = aff_ref.shape[0]
    bits = pltpu.bitcast(aff_ref[...], jnp.int32)

    def count(pred):
        return jnp.sum(jnp.sum(pred.astype(F32), axis=0), axis=1, keepdims=True)

    def bisect(_, lo_hi):
        lo, hi = lo_hi
        mid = lo + ((hi - lo + 1) >> 1)
        ok = count(bits >= mid[None]) >= cap
        return jnp.where(ok, mid, lo), jnp.where(ok, hi, mid - 1)

    cut, _ = lax.fori_loop(0, BISECT_STEPS, bisect,
                           (jnp.zeros((N_EXPERTS, 1), jnp.int32),
                            jnp.full((N_EXPERTS, 1), ONE_F32_BITS, jnp.int32)))
    ties_wanted = cap - count(bits > cut[None])

    def scan(c, carry):
        n_sel, n_tie = carry
        b = pltpu.bitcast(aff_ref[c], jnp.int32)
        above = b > cut
        tie = (b == cut).astype(F32)
        tie_rank = n_tie + jnp.dot(tie.astype(BF16), tri_ref[...], preferred_element_type=F32) - tie
        sel = jnp.where(above, 1.0, jnp.where(tie_rank < ties_wanted, tie, 0.0))
        slot = n_sel + jnp.dot(sel.astype(BF16), tri_ref[...], preferred_element_type=F32) - sel
        slot_ref[c] = jnp.where(sel > 0.0, slot, -1.0).astype(jnp.int32)
        off_ref[c] = jnp.broadcast_to(n_sel, (N_EXPERTS, ROUTE_CHUNK)).astype(jnp.int32)
        return (n_sel + jnp.sum(sel, axis=1, keepdims=True), n_tie + jnp.sum(tie, axis=1, keepdims=True))

    zero = jnp.zeros((N_EXPERTS, 1), F32)
    lax.fori_loop(0, n_chunks, scan, (zero, zero))


@functools.lru_cache(maxsize=None)
def _prefix_table():
    t = np.arange(ROUTE_CHUNK)
    return jnp.asarray((t[:, None] <= t[None, :]).astype(np.float32), dtype=BF16)


def _select(aff, cap):
    n_chunks = aff.shape[0]
    sds = jax.ShapeDtypeStruct((n_chunks, N_EXPERTS, ROUTE_CHUNK), jnp.int32)
    return pl.pallas_call(
        functools.partial(_select_kernel, cap=cap),
        out_shape=[sds, sds],
        compiler_params=pltpu.CompilerParams(vmem_limit_bytes=V7X_VMEM_LIMIT_BYTES),
        name="route_select",
    )(aff, _prefix_table())


PLACE_ROWS = 8


def _place_kernel(off_sm, slot_ref, aff_ref, idx_ref, gate_ref, acc_ref):
    n_chunks = slot_ref.shape[0]
    n_tiles = idx_ref.shape[1]
    acc_ref[...] = jnp.zeros_like(acc_ref)
    lane = lax.broadcasted_iota(jnp.int32, (1, ROUTE_CHUNK), 1)
    j_iota = lax.broadcasted_iota(jnp.int32, (ROUTE_CHUNK, ROUTE_CHUNK), 0)
    lane8 = lax.broadcasted_iota(jnp.int32, (PLACE_ROWS, ROUTE_CHUNK), 1)
    lo_row = lane.astype(F32)

    def chunk(c, carry):
        slots = slot_ref[c]
        aff = aff_ref[c]
        hi_row = jnp.full((1, ROUTE_CHUNK), c, jnp.int32).astype(F32)
        for e in range(N_EXPERTS):
            off = off_sm[c * N_EXPERTS + e]
            tile = off >> 7
            shift = off & (ROUTE_CHUNK - 1)
            local = slots[e:e + 1] - off
            onehot = jnp.where(local == j_iota, 1.0, 0.0).astype(BF16)
            a = aff[e:e + 1]
            g1 = a.astype(BF16)
            g2 = (a - g1.astype(F32)).astype(BF16)
            g3 = (a - g1.astype(F32) - g2.astype(F32)).astype(BF16)
            lhs = jnp.concatenate([hi_row.astype(BF16), lo_row.astype(BF16), g1, g2, g3,
                                   jnp.zeros((PLACE_ROWS - 5, ROUTE_CHUNK), BF16)], axis=0)
            packed = lax.dot_general(lhs, onehot, (((1,), (1,)), ((), ())), preferred_element_type=F32)
            rolled = pltpu.roll(packed, shift, 1)
            acc_ref[e, tile] = acc_ref[e, tile] + jnp.where(lane8 >= shift, rolled, 0.0)
            acc_ref[e, tile + 1] = acc_ref[e, tile + 1] + jnp.where(lane8 < shift, rolled, 0.0)
        return carry

    lax.fori_loop(0, n_chunks, chunk, 0)
    for e in range(N_EXPERTS):
        hi = acc_ref[e, :, 0, :]
        lo = acc_ref[e, :, 1, :]
        idx_ref[e] = (hi * float(ROUTE_CHUNK) + lo).astype(jnp.int32)[:n_tiles]
        gate_ref[e] = (acc_ref[e, :, 2, :] + acc_ref[e, :, 3, :] + acc_ref[e, :, 4, :])[:n_tiles]


def _place(offsets_flat, slots, aff, cap):
    n_tiles = cap // ROUTE_CHUNK
    return pl.pallas_call(
        _place_kernel,
        grid_spec=pltpu.PrefetchScalarGridSpec(
            num_scalar_prefetch=1,
            grid=(1,),
            in_specs=[pl.BlockSpec(slots.shape, lambda i, off: (0, 0, 0)),
                      pl.BlockSpec(aff.shape, lambda i, off: (0, 0, 0))],
            out_specs=[pl.BlockSpec((N_EXPERTS, n_tiles, ROUTE_CHUNK), lambda i, off: (0, 0, 0))] * 2,
            scratch_shapes=[pltpu.VMEM((N_EXPERTS, n_tiles + 2, PLACE_ROWS, ROUTE_CHUNK), F32)],
        ),
        out_shape=[jax.ShapeDtypeStruct((N_EXPERTS, n_tiles, ROUTE_CHUNK), jnp.int32),
                   jax.ShapeDtypeStruct((N_EXPERTS, n_tiles, ROUTE_CHUNK), F32)],
        compiler_params=_cparams(("arbitrary",)),
        name="route_place",
    )(offsets_flat, slots, aff)


FFN_CHUNKS = 8
FFN_BUFFERS = 3


def _ffn_kernel(idx_sm, h_hbm, gate_ref, wg_ref, wu_ref, wd_ref, y_ref, xbuf, sem):
    n_k = pl.num_programs(1)
    step = pl.program_id(0) * n_k + pl.program_id(1)
    n_steps = pl.num_programs(0) * n_k
    slot = step % FFN_BUFFERS

    def row_copy(tile, buf, j):
        row = idx_sm[tile * SLOT_TILE + j]
        return pltpu.make_async_copy(h_hbm.at[pl.ds(row, 1)], xbuf.at[buf, pl.ds(j, 1)], sem.at[buf])

    def gather(tile, buf):
        def body(j, carry):
            row_copy(tile, buf, j).start()
            return carry
        lax.fori_loop(0, SLOT_TILE, body, 0)

    @pl.when(step == 0)
    def _():
        gather(0, 0)

    @pl.when((step == 0) & (n_steps > 1))
    def _():
        gather(1, 1)

    pltpu.make_async_copy(h_hbm.at[pl.ds(0, SLOT_TILE)], xbuf.at[slot], sem.at[slot]).wait()
    x = xbuf[slot].astype(BF16)

    width = D_FF_EXPERT // FFN_CHUNKS
    acc = None
    for f in range(FFN_CHUNKS):
        cols = slice(f * width, (f + 1) * width)
        hg = jnp.dot(x, wg_ref[:, cols], preferred_element_type=F32)
        hu = jnp.dot(x, wu_ref[:, cols], preferred_element_type=F32)
        he = (hg * _sigmoid(hg) * hu).astype(BF16)
        part = jnp.dot(he, wd_ref[cols, :], preferred_element_type=F32)
        acc = part if acc is None else acc + part
    y_ref[...] = acc * gate_ref[...]

    @pl.when(step + 2 < n_steps)
    def _():
        for j in range(SLOT_TILE):
            row_copy(step + 2, (step + 2) % FFN_BUFFERS, j).start()


def _ffn(idx_flat, h, gate_col, wg_bf16, wu_bf16, wd_bf16, layer, cap):
    assert cap % SLOT_TILE == 0
    n_k = cap // SLOT_TILE
    wspec = lambda shape: pl.BlockSpec((None, None) + shape, lambda e, k, idx: (layer, e, 0, 0))
    tile = lambda e, k, idx: (e * n_k + k, 0)
    return pl.pallas_call(
        _ffn_kernel,
        grid_spec=pltpu.PrefetchScalarGridSpec(
            num_scalar_prefetch=1,
            grid=(N_EXPERTS, n_k),
            in_specs=[pl.BlockSpec(memory_space=pl.ANY),
                      pl.BlockSpec((SLOT_TILE, 1), tile),
                      wspec((D_MODEL, D_FF_EXPERT)), wspec((D_MODEL, D_FF_EXPERT)),
                      wspec((D_FF_EXPERT, D_MODEL))],
            out_specs=pl.BlockSpec((SLOT_TILE, D_MODEL), tile),
            scratch_shapes=[pltpu.VMEM((FFN_BUFFERS, SLOT_TILE, D_MODEL), F32),
                            pltpu.SemaphoreType.DMA((FFN_BUFFERS,))],
        ),
        out_shape=jax.ShapeDtypeStruct((N_EXPERTS * cap, D_MODEL), F32),
        compiler_params=_cparams(("arbitrary", "arbitrary")),
        name="expert_ffn",
    )(idx_flat, h, gate_col, wg_bf16, wu_bf16, wd_bf16)


PIECE = 32
SLOT_SPLIT_BITS = 6
SLOT_SPLIT = 1 << SLOT_SPLIT_BITS
PIECES_PER_EXPERT = -(-(ROUTE_CHUNK + 2 * (V7X_SUBLANES - 1)) // PIECE)
PACK_ROWS = N_EXPERTS * PIECES_PER_EXPERT * PIECE
PACK_BLOCK = 256


def _combine_kernel(off_sm, x_ref, slot_ref, y_hbm, o_ref, buf, sem, *, cap):
    i = pl.program_id(0)
    n_tiles = pl.num_programs(0)
    cur = i % 2

    def plan(tile):
        starts, pieces, bases = [], [], []
        base = 0
        for e in range(N_EXPERTS):
            s0 = off_sm[tile * N_EXPERTS + e]
            s1 = off_sm[(tile + 1) * N_EXPERTS + e]
            a0 = (s0 >> 3) << 3
            rows = jnp.where(s1 > s0, (((s1 + V7X_SUBLANES - 1) >> 3) << 3) - a0, 0)
            n_p = (rows + PIECE - 1) // PIECE
            starts.append(jnp.minimum(a0, cap - n_p * PIECE))
            pieces.append(n_p)
            bases.append(base)
            base = base + n_p * PIECE
        return starts, pieces, bases, base

    def piece_copy(src_row, b, dst_row):
        return pltpu.make_async_copy(y_hbm.at[pl.ds(pl.multiple_of(src_row, V7X_SUBLANES), PIECE)],
                                     buf.at[b, pl.ds(pl.multiple_of(dst_row, V7X_SUBLANES), PIECE)], sem.at[b])

    def fetch(tile, b):
        starts, pieces, bases, _ = plan(tile)
        for e in range(N_EXPERTS):
            def body(j, carry, e=e):
                piece_copy(e * cap + starts[e] + j * PIECE, b, bases[e] + j * PIECE).start()
                return carry
            lax.fori_loop(0, pieces[e], body, 0)

    @pl.when(i == 0)
    def _():
        buf[...] = jnp.zeros_like(buf)
        fetch(0, 0)

    @pl.when(i + 1 < n_tiles)
    def _():
        fetch(i + 1, 1 - cur)

    starts, pieces, bases, total = plan(i)

    def wait_piece(j, carry):
        piece_copy(0, cur, 0).wait()
        return carry
    lax.fori_loop(0, total // PIECE, wait_piece, 0)

    slot1 = slot_ref[...] + 1
    parts = jnp.concatenate([(slot1 >> SLOT_SPLIT_BITS).astype(F32), (slot1 & (SLOT_SPLIT - 1)).astype(F32),
                             jnp.zeros((ROUTE_CHUNK - 2 * N_EXPERTS, ROUTE_CHUNK), F32)], axis=0)
    slot_parts = parts.T[:, :2 * N_EXPERTS].astype(BF16)
    e_iota = lax.broadcasted_iota(jnp.int32, (N_EXPERTS, 1), 0)
    first = jnp.zeros((N_EXPERTS, 1), F32)
    last = jnp.zeros((N_EXPERTS, 1), F32)
    shift = jnp.zeros((N_EXPERTS, 1), F32)
    as_f32 = lambda v: jnp.asarray(v, jnp.int32).astype(F32)
    for e in range(N_EXPERTS):
        here = e_iota == e
        first = jnp.where(here, as_f32(bases[e]), first)
        last = jnp.where(here, as_f32(bases[e] + pieces[e] * PIECE), last)
        shift = jnp.where(here, as_f32(bases[e] - starts[e]), shift)

    o_ref[...] = x_ref[...]
    lane = lax.broadcasted_iota(jnp.int32, (1, PACK_BLOCK), 1).astype(F32)

    def block(kb, carry):
        rows = pl.ds(pl.multiple_of(kb * PACK_BLOCK, PACK_BLOCK), PACK_BLOCK)
        row_id = lane + (kb * PACK_BLOCK).astype(F32)
        member = jnp.where((row_id >= first) & (row_id < last), 1.0, 0.0)
        spread = jnp.concatenate([member * float(SLOT_SPLIT), member], axis=0).astype(BF16)
        slot_of_row = jnp.dot(slot_parts, spread, preferred_element_type=F32)
        wanted = row_id - jnp.sum(member * shift, axis=0, keepdims=True) + 1.0
        onehot = jnp.where(slot_of_row == wanted, 1.0, 0.0).astype(BF16)
        y = buf[cur, rows, :]
        y_hi = y.astype(BF16)
        y_lo = (y - y_hi.astype(F32)).astype(BF16)
        o_ref[...] = (o_ref[...] + jnp.dot(onehot, y_hi, preferred_element_type=F32)
                      + jnp.dot(onehot, y_lo, preferred_element_type=F32))
        return carry
    lax.fori_loop(0, (total + PACK_BLOCK - 1) // PACK_BLOCK, block, 0)


def _combine(offsets_flat, x, slots, ye, cap):
    assert cap < SLOT_SPLIT * 256 and cap % PIECE == 0
    n_tok = x.shape[0]
    tok = pl.BlockSpec((ROUTE_CHUNK, D_MODEL), lambda i, off: (i, 0))
    return pl.pallas_call(
        functools.partial(_combine_kernel, cap=cap),
        grid_spec=pltpu.PrefetchScalarGridSpec(
            num_scalar_prefetch=1,
            grid=(n_tok // ROUTE_CHUNK,),
            in_specs=[tok,
                      pl.BlockSpec((None, N_EXPERTS, ROUTE_CHUNK), lambda i, off: (i, 0, 0)),
                      pl.BlockSpec(memory_space=pl.ANY)],
            out_specs=tok,
            scratch_shapes=[pltpu.VMEM((2, PACK_ROWS, D_MODEL), F32), pltpu.SemaphoreType.DMA((2,))],
        ),
        out_shape=jax.ShapeDtypeStruct((n_tok, D_MODEL), F32),
        compiler_params=_cparams(("arbitrary",)),
        name="combine",
    )(offsets_flat, x, slots, ye)


def _final_norm_kernel(x_ref, nw_ref, o_ref):
    x = x_ref[...]
    ms = jnp.mean(x * x, axis=-1, keepdims=True)
    o_ref[...] = x * lax.rsqrt(ms + RMS_EPS) * nw_ref[...]


def _final_norm(x, norm_w):
    n_tok = x.shape[0]
    tm = min(ROW_TILE, n_tok)
    return pl.pallas_call(
        _final_norm_kernel,
        grid=(n_tok // tm,),
        in_specs=[pl.BlockSpec((tm, D_MODEL), lambda i: (i, 0)), pl.BlockSpec((1, D_MODEL), lambda i: (0, 0))],
        out_specs=pl.BlockSpec((tm, D_MODEL), lambda i: (i, 0)),
        out_shape=jax.ShapeDtypeStruct((n_tok, D_MODEL), F32),
        compiler_params=_cparams(("parallel",)),
        name="final_norm",
    )(x, norm_w)


def _moe(x2, h2, aff, params, layer):
    n_tok = x2.shape[0]
    cap = EC_CAPACITY_FACTOR * n_tok // N_EXPERTS
    slots, offs = _select(aff, cap)
    offsets_flat = jnp.concatenate(
        [offs[:, :, 0], jnp.full((1, N_EXPERTS), cap, jnp.int32)], axis=0).reshape(-1)
    idx, gate = _place(offsets_flat, slots, aff, cap)
    ye = _ffn(idx.reshape(-1), h2, gate.reshape(-1, 1), params["w_gate"], params["w_up"], params["w_down"],
              layer, cap)
    return _combine(offsets_flat, x2, slots, ye, cap)


def _trunk(x, params):
    bsz, seq, _ = x.shape
    x = x.reshape(bsz * seq, D_MODEL)
    for layer in range(DEPTH):
        u, q, kf, kb, lf, lb, v, g = _inproj(x, params["norm_mix"], params["w_in"], params["lbf"], params["lbb"], layer)
        yf = _fnet(u, params["w_fnet"], layer, bsz, seq)
        yr = _hgrn(q, kf, kb, lf, lb, v, g, params["hgrn_out_norm"], layer, bsz, seq)
        x2, h2, aff = _outproj(yf, yr, x, params["w_out"], params["norm_ffn"], params["w_router_t"], layer)
        x = _moe(x2, h2, aff, params, layer)
    return _final_norm(x, params["norm_final"]).reshape(bsz, seq, D_MODEL)


def kernel(x_prompt, x_sample, norm_mix, w_in, w_fnet, lb_fwd, lb_bwd, hgrn_out_norm, w_out,
           norm_ffn, w_router, w_gate, w_up, w_down, norm_final):
    params = {
        "norm_mix": norm_mix.reshape(DEPTH, 1, D_MODEL),
        "w_in": w_in.astype(BF16),
        "w_fnet": w_fnet.astype(BF16),
        "lbf": _lower_bounds(lb_fwd).reshape(DEPTH, 1, QK_WIDTH),
        "lbb": _lower_bounds(lb_bwd).reshape(DEPTH, 1, QK_WIDTH),
        "hgrn_out_norm": hgrn_out_norm.reshape(DEPTH, 1, HGRN_DV),
        "w_out": w_out.astype(BF16),
        "norm_ffn": norm_ffn.reshape(DEPTH, 1, D_MODEL),
        "w_router_t": jnp.swapaxes(w_router, 1, 2),
        "w_gate": w_gate.astype(BF16),
        "w_up": w_up.astype(BF16),
        "w_down": w_down.astype(BF16),
        "norm_final": norm_final.reshape(1, D_MODEL),
    }
    return _trunk(x_prompt, params), _trunk(x_sample, params)
```
